```python
import math
import jax, jax.numpy as jnp
from jax import lax
import numpy as np

D_MODEL = 2048
BATCH = 1
SEQ = 8192
DEPTH = 4
DEC_BATCH = 4
DEC_SEQ = 4096
PAST_LEN = 128

HEAD_DIM = 128
N_HEADS_A = 8
N_HEADS_B = 8
WIDTH_A = N_HEADS_A * HEAD_DIM
DQ_B = HEAD_DIM // 2
WIDTH_B = N_HEADS_B * HEAD_DIM
MIX_WIDTH = WIDTH_A + WIDTH_B
IN_WIDTH = 3 * WIDTH_A + 2 * (N_HEADS_B * 2 * DQ_B) + WIDTH_B
DILATED_CONFIGS = ((128, 1), (512, 4), (2048, 16))
D_FF = 5632
Q_BLOCK = 128
ALIBI_MAX_BIAS = 8.0
DEEPNORM_ALPHA = (2 * DEPTH) ** 0.25
DEEPNORM_BETA = (8 * DEPTH) ** -0.25
LN_EPS = 1e-5
RMS_EPS = 1e-6
NEG_INF = -1e30

kernel_name = "hymba_longnet_diffattn_macaron_deepnorm_encoder"


def layer_norm(x, g, b):
    xf = x.astype(jnp.float32)
    mu = xf.mean(-1, keepdims=True)
    var = jnp.square(xf - mu).mean(-1, keepdims=True)
    y = (xf - mu) * lax.rsqrt(var + LN_EPS)
    return (y * g.astype(jnp.float32) + b.astype(jnp.float32)).astype(x.dtype)


def rms_norm(x, g):
    xf = x.astype(jnp.float32)
    y = xf * lax.rsqrt(jnp.square(xf).mean(-1, keepdims=True) + RMS_EPS)
    return (y * g.astype(jnp.float32)).astype(x.dtype)


def swiglu(x, w_gate, w_up, w_down):
    return (jax.nn.silu(x @ w_gate) * (x @ w_up)) @ w_down


def alibi_slopes():
    n = N_HEADS_A + N_HEADS_B
    s = jnp.exp2(-ALIBI_MAX_BIAS * jnp.arange(1, n + 1, dtype=jnp.float32) / n)
    return s[0::2], s[1::2]


def dilated_branch(q, k, v, slopes, window, dil):
    B, S, H, D = q.shape
    R = (window // 2) // dil
    L = S // dil
    nb = -(-L // R)
    Lp = nb * R

    def to_sub(t):
        return t.reshape(B, L, dil, H, D).transpose(0, 2, 1, 3, 4)

    qb = jnp.pad(to_sub(q), ((0, 0), (0, 0), (0, Lp - L), (0, 0), (0, 0)))
    qb = qb.reshape(B, dil, nb, R, H, D)
    pad_kv = ((0, 0), (0, 0), (R, Lp - L + R), (0, 0), (0, 0))
    kp = jnp.pad(to_sub(k), pad_kv)
    vp = jnp.pad(to_sub(v), pad_kv)

    def band(t):
        return jnp.concatenate(
            [t[:, :, o * R:o * R + Lp].reshape(B, dil, nb, R, H, D) for o in range(3)], axis=3)

    kb, vb = band(kp), band(vp)
    s = jnp.einsum('bgnqhd,bgnkhd->bgnhqk', qb, kb,
                   preferred_element_type=jnp.float32) * (D ** -0.5)
    qi = jnp.arange(R)[:, None]
    kj = jnp.arange(3 * R)[None, :]
    delta = kj - R - qi
    key_idx = jnp.arange(nb)[:, None, None] * R - R + kj[None]
    valid = (jnp.abs(delta) <= R)[None] & (key_idx >= 0) & (key_idx < L)
    dist = (jnp.abs(delta) * dil).astype(jnp.float32)
    s = s - slopes[:, None, None] * dist
    s = jnp.where(valid[:, None], s, NEG_INF)
    m = s.max(-1, keepdims=True)
    e = jnp.exp(s - m)
    den = e.sum(-1, keepdims=True)
    o = jnp.einsum('bgnhqk,bgnkhd->bgnqhd', (e / den).astype(v.dtype), vb)
    lse = (m + jnp.log(den))[..., 0].transpose(0, 1, 2, 4, 3)
    o = o.reshape(B, dil, Lp, H, D)[:, :, :L].transpose(0, 2, 1, 3, 4).reshape(B, S, H, D)
    lse = lse.reshape(B, dil, Lp, H)[:, :, :L].transpose(0, 2, 1, 3).reshape(B, S, H)
    return o, lse


def dilated_attention(q, k, v, slopes):
    outs, lses = [], []
    for window, dil in DILATED_CONFIGS:
        o, l = dilated_branch(q, k, v, slopes, window, dil)
        outs.append(o)
        lses.append(l)
    w = jax.nn.softmax(jnp.stack(lses, 0), axis=0)
    out = jnp.sum(w[..., None] * jnp.stack(outs, 0).astype(jnp.float32), axis=0)
    return out.astype(q.dtype)


def diff_attention(q, k, v, slopes, lam):
    B, S, H, _, DQ = q.shape
    nq = S // Q_BLOCK
    qb = q.reshape(B, nq, Q_BLOCK, H, 2, DQ).transpose(1, 0, 2, 3, 4, 5)
    starts = jnp.arange(nq) * Q_BLOCK
    kpos = jnp.arange(S)

    def block(args):
        qblk, start = args
        s = jnp.einsum('bqhcd,bkhcd->bhcqk', qblk, k,
                       preferred_element_type=jnp.float32) * (DQ ** -0.5)
        dist = jnp.abs(start + jnp.arange(Q_BLOCK)[:, None] - kpos[None, :]).astype(jnp.float32)
        s = s - slopes[:, None, None, None] * dist
        p = jax.nn.softmax(s, axis=-1)
        a = p[:, :, 0] - lam * p[:, :, 1]
        return jnp.einsum('bhqk,bkhd->bqhd', a.astype(v.dtype), v)

    o = lax.map(block, (qb, starts))
    return o.transpose(1, 0, 2, 3, 4).reshape(B, S, H, 2 * DQ)


def setup_inputs(seed: int = 0) -> dict:
    key = jax.random.key(seed)
    ks = jax.random.split(key, 24)
    f32 = jnp.float32
    nrm = lambda k, shape, scale: jax.random.normal(k, shape, f32) * scale
    gain = lambda k, shape: 1.0 + 0.01 * jax.random.normal(k, shape, f32)
    return {
        "x_prompt": jax.random.normal(ks[0], (BATCH, SEQ, D_MODEL), f32),
        "x_sample": jax.random.normal(ks[1], (DEC_BATCH, DEC_SEQ, D_MODEL), f32),
        "ln1_g": gain(ks[2], (DEPTH, D_MODEL)),
        "ln1_b": nrm(ks[3], (DEPTH, D_MODEL), 0.01),
        "ffn1_gate": nrm(ks[4], (DEPTH, D_MODEL, D_FF), D_MODEL ** -0.5),
        "ffn1_up": nrm(ks[5], (DEPTH, D_MODEL, D_FF), D_MODEL ** -0.5),
        "ffn1_down": nrm(ks[6], (DEPTH, D_FF, D_MODEL), D_FF ** -0.5 * DEEPNORM_BETA),
        "w_in": nrm(ks[7], (DEPTH, D_MODEL, IN_WIDTH), D_MODEL ** -0.5),
        "norm_a_g": gain(ks[8], (DEPTH, HEAD_DIM)),
        "lam_q1": nrm(ks[9], (DEPTH, DQ_B), 0.1),
        "lam_k1": nrm(ks[10], (DEPTH, DQ_B), 0.1),
        "lam_q2": nrm(ks[11], (DEPTH, DQ_B), 0.1),
        "lam_k2": nrm(ks[12], (DEPTH, DQ_B), 0.1),
        "subln_g": gain(ks[13], (DEPTH, HEAD_DIM)),
        "w_out": nrm(ks[14], (DEPTH, MIX_WIDTH, D_MODEL), MIX_WIDTH ** -0.5 * DEEPNORM_BETA),
        "ln2_g": gain(ks[15], (DEPTH, D_MODEL)),
        "ln2_b": nrm(ks[16], (DEPTH, D_MODEL), 0.01),
        "ffn2_gate": nrm(ks[17], (DEPTH, D_MODEL, D_FF), D_MODEL ** -0.5),
        "ffn2_up": nrm(ks[18], (DEPTH, D_MODEL, D_FF), D_MODEL ** -0.5),
        "ffn2_down": nrm(ks[19], (DEPTH, D_FF, D_MODEL), D_FF ** -0.5 * DEEPNORM_BETA),
        "ln3_g": gain(ks[20], (DEPTH, D_MODEL)),
        "ln3_b": nrm(ks[21], (DEPTH, D_MODEL), 0.01),
    }


def reference(x_prompt, x_sample, ln1_g, ln1_b, ffn1_gate, ffn1_up, ffn1_down, w_in,
              norm_a_g, lam_q1, lam_k1, lam_q2, lam_k2, subln_g, w_out, ln2_g, ln2_b,
              ffn2_gate, ffn2_up, ffn2_down, ln3_g, ln3_b):
    slopes_a, slopes_b = alibi_slopes()

    def mixer(x, i):
        B, S, _ = x.shape
        h = x @ w_in[i]
        o0 = 3 * WIDTH_A
        qk_w = N_HEADS_B * 2 * DQ_B
        qa = h[..., 0:WIDTH_A].reshape(B, S, N_HEADS_A, HEAD_DIM)
        ka = h[..., WIDTH_A:2 * WIDTH_A].reshape(B, S, N_HEADS_A, HEAD_DIM)
        va = h[..., 2 * WIDTH_A:o0].reshape(B, S, N_HEADS_A, HEAD_DIM)
        qb = h[..., o0:o0 + qk_w].reshape(B, S, N_HEADS_B, 2, DQ_B)
        kb = h[..., o0 + qk_w:o0 + 2 * qk_w].reshape(B, S, N_HEADS_B, 2, DQ_B)
        vb = h[..., o0 + 2 * qk_w:].reshape(B, S, N_HEADS_B, 2 * DQ_B)
        ya = rms_norm(dilated_attention(qa, ka, va, slopes_a), norm_a_g[i])
        lam_init = 0.8 - 0.6 * math.exp(-0.3 * i)
        lam = (jnp.exp(jnp.sum(lam_q1[i].astype(jnp.float32) * lam_k1[i].astype(jnp.float32)))
               - jnp.exp(jnp.sum(lam_q2[i].astype(jnp.float32) * lam_k2[i].astype(jnp.float32)))
               + lam_init)
        yb = rms_norm(diff_attention(qb, kb, vb, slopes_b, lam), subln_g[i]) * (1.0 - lam_init)
        y = jnp.concatenate([ya.reshape(B, S, WIDTH_A), yb.reshape(B, S, WIDTH_B)], axis=-1)
        return y @ w_out[i]

    def trunk(x):
        for i in range(DEPTH):
            x = layer_norm(DEEPNORM_ALPHA * x + 0.5 * swiglu(x, ffn1_gate[i], ffn1_up[i], ffn1_down[i]),
                           ln1_g[i], ln1_b[i])
            x = layer_norm(DEEPNORM_ALPHA * x + mixer(x, i), ln2_g[i], ln2_b[i])
            x = layer_norm(DEEPNORM_ALPHA * x + 0.5 * swiglu(x, ffn2_gate[i], ffn2_up[i], ffn2_down[i]),
                           ln3_g[i], ln3_b[i])
        return x

    y_prompt = trunk(x_prompt)
    y_sample = trunk(x_sample)
    return (y_prompt, y_sample)
```

```python
import functools
import math

import jax
import jax.numpy as jnp
from jax import lax
from jax.experimental import pallas as pl
from jax.experimental.pallas import tpu as pltpu

F32 = jnp.float32
BF16 = jnp.bfloat16

HEAD_DIM = 128
N_HEADS_A = 8
N_HEADS_B = 8
DQ_B = HEAD_DIM // 2
GROUP_WIDTH = N_HEADS_A * HEAD_DIM
N_GROUPS = 6
DILATED_CONFIGS = ((128, 1), (512, 4), (2048, 16))
ALIBI_MAX_BIAS = 8.0
LN_EPS = 1e-5
RMS_EPS = 1e-6
NEG_INF = -1e30

MIB = 1024 * 1024


def _alibi_slope(n):
    return 2.0 ** (-ALIBI_MAX_BIAS * n / (N_HEADS_A + N_HEADS_B))


SLOPES_A = tuple(_alibi_slope(2 * h + 1) for h in range(N_HEADS_A))
SLOPES_B = tuple(_alibi_slope(2 * h + 2) for h in range(N_HEADS_B))


def _layer_norm(y, g, b):
    mu = jnp.mean(y, axis=-1, keepdims=True)
    d = y - mu
    var = jnp.mean(d * d, axis=-1, keepdims=True)
    return d * lax.rsqrt(var + LN_EPS) * g + b


def _rms_norm(y, g):
    return y * lax.rsqrt(jnp.mean(y * y, axis=-1, keepdims=True) + RMS_EPS) * g


def _ffn_body(x_ref, wg_ref, wu_ref, wd_ref, g_ref, b_ref, o_ref, xb_ref, acc_ref, *, alpha):
    j = pl.program_id(1)

    @pl.when(j == 0)
    def _():
        xb_ref[...] = x_ref[...].astype(BF16)
        acc_ref[...] = jnp.zeros_like(acc_ref)

    xb = xb_ref[...]
    gate = jnp.dot(xb, wg_ref[...], preferred_element_type=F32)
    up = jnp.dot(xb, wu_ref[...], preferred_element_type=F32)
    act = (gate * jax.nn.sigmoid(gate) * up).astype(BF16)
    acc_ref[...] += jnp.dot(act, wd_ref[...], preferred_element_type=F32)

    @pl.when(j == pl.num_programs(1) - 1)
    def _():
        y = alpha * x_ref[...] + 0.5 * acc_ref[...]
        o_ref[...] = _layer_norm(y, g_ref[...], b_ref[...])


def _ffn(x, wg, wu, wd, g, b, layer, alpha, tm, tf):
    t, d = x.shape
    f = wg.shape[-1]
    est = 2 * 2 * tm * d * 4 + 2 * 3 * d * tf * 2 + tm * d * 6 + 4 * tm * tf * 4
    return pl.pallas_call(
        functools.partial(_ffn_body, alpha=alpha),
        grid=(t // tm, f // tf),
        in_specs=[
            pl.BlockSpec((tm, d), lambda i, j: (i, 0)),
            pl.BlockSpec((None, d, tf), lambda i, j: (layer, 0, j)),
            pl.BlockSpec((None, d, tf), lambda i, j: (layer, 0, j)),
            pl.BlockSpec((None, tf, d), lambda i, j: (layer, j, 0)),
            pl.BlockSpec((None, 1, d), lambda i, j: (layer, 0, 0)),
            pl.BlockSpec((None, 1, d), lambda i, j: (layer, 0, 0)),
        ],
        out_specs=pl.BlockSpec((tm, d), lambda i, j: (i, 0)),
        out_shape=jax.ShapeDtypeStruct((t, d), F32),
        scratch_shapes=[pltpu.VMEM((tm, d), BF16), pltpu.VMEM((tm, d), F32)],
        compiler_params=pltpu.CompilerParams(
            dimension_semantics=("parallel", "arbitrary"), vmem_limit_bytes=_vmem_limit(est)),
        name="ffn",
    )(x, wg, wu, wd, g, b)


def _in_proj_body(x_ref, w_ref, o_ref, xb_ref):
    @pl.when(pl.program_id(1) == 0)
    def _():
        xb_ref[...] = x_ref[...].astype(BF16)

    o_ref[...] = jnp.dot(xb_ref[...], w_ref[...], preferred_element_type=F32).astype(BF16)


def _in_proj(x, w, layer, tm):
    t, d = x.shape
    n = w.shape[-1]
    tn = GROUP_WIDTH
    est = 2 * tm * d * 4 + 2 * d * tn * 2 + 2 * tm * tn * 2 + tm * d * 2 + tm * tn * 4
    return pl.pallas_call(
        _in_proj_body,
        grid=(t // tm, n // tn),
        in_specs=[
            pl.BlockSpec((tm, d), lambda i, j: (i, 0)),
            pl.BlockSpec((None, d, tn), lambda i, j: (layer, 0, j)),
        ],
        out_specs=pl.BlockSpec((tm, tn), lambda i, j: (i, j)),
        out_shape=jax.ShapeDtypeStruct((t, n), BF16),
        scratch_shapes=[pltpu.VMEM((tm, d), BF16)],
        compiler_params=pltpu.CompilerParams(
            dimension_semantics=("parallel", "arbitrary"), vmem_limit_bytes=_vmem_limit(est)),
        name="in_proj",
    )(x, w)


def _out_proj_body(x_ref, ya_ref, yb_ref, w_ref, g_ref, b_ref, o_ref, *, alpha):
    wa = ya_ref.shape[-1]
    y = jnp.dot(ya_ref[...], w_ref[:wa, :], preferred_element_type=F32)
    y = y + jnp.dot(yb_ref[...], w_ref[wa:, :], preferred_element_type=F32)
    o_ref[...] = _layer_norm(alpha * x_ref[...] + y, g_ref[...], b_ref[...])


def _out_proj(x, ya, yb, w, g, b, layer, alpha, tm):
    t, d = x.shape
    wa, wb = ya.shape[-1], yb.shape[-1]
    est = 2 * 2 * tm * d * 4 + 2 * tm * (wa + wb) * 2 + 2 * (wa + wb) * d * 2 + 2 * tm * d * 4
    return pl.pallas_call(
        functools.partial(_out_proj_body, alpha=alpha),
        grid=(t // tm,),
        in_specs=[
            pl.BlockSpec((tm, d), lambda i: (i, 0)),
            pl.BlockSpec((tm, wa), lambda i: (i, 0)),
            pl.BlockSpec((tm, wb), lambda i: (i, 0)),
            pl.BlockSpec((None, wa + wb, d), lambda i: (layer, 0, 0)),
            pl.BlockSpec((None, 1, d), lambda i: (layer, 0, 0)),
            pl.BlockSpec((None, 1, d), lambda i: (layer, 0, 0)),
        ],
        out_specs=pl.BlockSpec((tm, d), lambda i: (i, 0)),
        out_shape=jax.ShapeDtypeStruct((t, d), F32),
        compiler_params=pltpu.CompilerParams(
            dimension_semantics=("parallel",), vmem_limit_bytes=_vmem_limit(est)),
        name="out_proj",
    )(x, ya, yb, w, g, b)


def _seq_bounds(t0, seqs):
    n_p, s_p, s_s = seqs
    p_tot = n_p * s_p
    in_prompt = t0 < p_tot
    lo_p = (t0 // s_p) * s_p
    lo_s = p_tot + ((jnp.maximum(t0, p_tot) - p_tot) // s_s) * s_s
    return jnp.where(in_prompt, lo_p, lo_s), jnp.where(in_prompt, s_p, s_s)


def _dilated_body(*refs, dil, radius, tq, first, last, seqs):
    q_ref, kp_ref, km_ref, kn_ref, vp_ref, vm_ref, vn_ref = refs[:7]
    pos = 7
    if not first:
        m_in, l_in, acc_in = refs[pos:pos + 3]
        pos += 3
    if last:
        ga_ref, y_ref = refs[pos:pos + 2]
        pos += 2
    else:
        m_out, l_out, acc_out = refs[pos:pos + 3]
        pos += 3
    kbuf, vbuf = refs[pos:pos + 2]

    n = pl.program_id(0)
    nk = tq + 2 * radius
    kbuf[0:radius, :] = kp_ref[...]
    kbuf[radius:radius + tq, :] = km_ref[...]
    kbuf[radius + tq:nk, :] = kn_ref[...]
    vbuf[0:radius, :] = vp_ref[...]
    vbuf[radius:radius + tq, :] = vm_ref[...]
    vbuf[radius + tq:nk, :] = vn_ref[...]

    lo, length = _seq_bounds(n * (tq * dil), seqs)
    lo_sub = lo // dil
    hi_sub = lo_sub + length // dil
    row = lax.broadcasted_iota(jnp.int32, (tq, nk), 0)
    col = lax.broadcasted_iota(jnp.int32, (tq, nk), 1)
    delta = col - radius - row
    key_row = n * tq - radius + col
    valid = (jnp.abs(delta) <= radius) & (key_row >= lo_sub) & (key_row < hi_sub)
    dist = (jnp.abs(delta) * dil).astype(F32)
    scale = HEAD_DIM ** -0.5
    lane = lax.broadcasted_iota(jnp.int32, (tq, HEAD_DIM), 1)

    m_tile = jnp.zeros((tq, HEAD_DIM), F32)
    l_tile = jnp.zeros((tq, HEAD_DIM), F32)
    for h in range(N_HEADS_A):
        hs = slice(h * HEAD_DIM, (h + 1) * HEAD_DIM)
        s = lax.dot_general(q_ref[:, hs], kbuf[:, hs], (((1,), (1,)), ((), ())),
                            preferred_element_type=F32) * scale
        s = jnp.where(valid, s - SLOPES_A[h] * dist, NEG_INF)
        m_b = jnp.max(s, axis=-1, keepdims=True)
        e = jnp.exp(s - m_b)
        l_b = jnp.sum(e, axis=-1, keepdims=True)
        a_b = jnp.dot(e.astype(BF16), vbuf[:, hs], preferred_element_type=F32)
        if first:
            m_new, l_new, a_new = m_b, l_b, a_b
        else:
            m_old = m_in[:, h:h + 1]
            m_new = jnp.maximum(m_old, m_b)
            w_old = jnp.exp(m_old - m_new)
            w_b = jnp.exp(m_b - m_new)
            l_new = l_in[:, h:h + 1] * w_old + l_b * w_b
            a_new = acc_in[:, hs] * w_old + a_b * w_b
        if last:
            y_ref[:, hs] = _rms_norm(a_new / l_new, ga_ref[...]).astype(y_ref.dtype)
        else:
            acc_out[:, hs] = a_new
            m_tile = jnp.where(lane == h, m_new, m_tile)
            l_tile = jnp.where(lane == h, l_new, l_tile)
    if not last:
        m_out[...] = m_tile
        l_out[...] = l_tile


def _dilated_branch(h, state, ga, layer, window, dil, tq, seqs, first, last):
    t = h.shape[0]
    w = GROUP_WIDTH
    radius = (window // 2) // dil
    rows = t // dil
    hv = h.reshape(rows, dil * h.shape[1])
    per = tq // radius
    n_halo = rows // radius
    col = lambda r, g: r * N_GROUPS + g
    main = lambda g: pl.BlockSpec((tq, w), lambda n, r: (n, col(r, g)))
    prev = lambda g: pl.BlockSpec((radius, w), lambda n, r: (jnp.maximum(n * per - 1, 0), col(r, g)))
    nxt = lambda g: pl.BlockSpec((radius, w), lambda n, r: (jnp.minimum((n + 1) * per, n_halo - 1), col(r, g)))
    stat = pl.BlockSpec((tq, HEAD_DIM), lambda n, r: (n, r))
    wide = pl.BlockSpec((tq, w), lambda n, r: (n, r))
    in_specs = [main(0), prev(1), main(1), nxt(1), prev(2), main(2), nxt(2)]
    args = [hv] * 7
    if not first:
        in_specs += [stat, stat, wide]
        args += [s.reshape(rows, -1) for s in state]
    if last:
        in_specs += [pl.BlockSpec((None, 1, HEAD_DIM), lambda n, r: (layer, 0, 0))]
        args += [ga]
        out_specs = wide
        out_shape = jax.ShapeDtypeStruct((rows, dil * w), BF16)
    else:
        out_specs = [stat, stat, wide]
        out_shape = [jax.ShapeDtypeStruct((rows, dil * HEAD_DIM), F32),
                     jax.ShapeDtypeStruct((rows, dil * HEAD_DIM), F32),
                     jax.ShapeDtypeStruct((rows, dil * w), F32)]
    nk = tq + 2 * radius
    est = 2 * 3 * nk * w * 2 + 2 * nk * w * 2 + 4 * tq * w * 4 + 8 * tq * nk * 4
    out = pl.pallas_call(
        functools.partial(_dilated_body, dil=dil, radius=radius, tq=tq, first=first, last=last, seqs=seqs),
        grid=(rows // tq, dil),
        in_specs=in_specs,
        out_specs=out_specs,
        out_shape=out_shape,
        scratch_shapes=[pltpu.VMEM((nk, w), BF16), pltpu.VMEM((nk, w), BF16)],
        compiler_params=pltpu.CompilerParams(
            dimension_semantics=("parallel", "parallel"), vmem_limit_bytes=_vmem_limit(est)),
        name=f"dilated_d{dil}",
    )(*args)
    if last:
        return out.reshape(t, w)
    return [o.reshape(t, -1) for o in out]


def _dilated_attention(h, ga, layer, tq, seqs):
    state = None
    nb = len(DILATED_CONFIGS)
    for b, (window, dil) in enumerate(DILATED_CONFIGS):
        state = _dilated_branch(h, state, ga, layer, window, dil, tq, seqs, b == 0, b == nb - 1)
    return state


def _diff_body(q_ref, klo_ref, khi_ref, vlo_ref, vhi_ref, slope_ref, lq1_ref, lk1_ref, lq2_ref, lk2_ref,
               g_ref, o_ref, qbd_ref, m_ref, l_ref, acc_ref, *, tq, tk, seqs, lam_init):
    n_p, s_p, s_s = seqs
    h = pl.program_id(0)
    i = pl.program_id(1)
    t0 = i * tq
    lo, length = _seq_bounds(t0, seqs)
    qpos0 = t0 - lo
    slope = slope_ref[pl.ds(h, 1), :][:, :1]

    q = q_ref[...]
    lane = lax.broadcasted_iota(jnp.int32, q.shape, 1)
    qs = q * jnp.asarray(DQ_B ** -0.5, q.dtype)
    zero = jnp.zeros_like(qs)
    qbd_ref[0:tq, :] = jnp.where(lane < DQ_B, qs, zero)
    qbd_ref[tq:2 * tq, :] = jnp.where(lane >= DQ_B, qs, zero)
    m_ref[...] = jnp.full_like(m_ref, NEG_INF)
    l_ref[...] = jnp.zeros_like(l_ref)
    acc_ref[...] = jnp.zeros_like(acc_ref)

    qpos = qpos0 + lax.broadcasted_iota(jnp.int32, (tq, tk), 0)
    kcol = lax.broadcasted_iota(jnp.int32, (tq, tk), 1)

    def tile(k_ref, v_ref, kbase):
        def body(j, carry):
            start = pl.multiple_of(j * tk, tk)
            kt = k_ref[pl.ds(start, tk), :]
            vt = v_ref[pl.ds(start, tk), :]
            s = lax.dot_general(qbd_ref[...], kt, (((1,), (1,)), ((), ())), preferred_element_type=F32)
            dist = jnp.abs(qpos - (kcol + (kbase + start))).astype(F32)
            s = s.reshape(2, tq, tk) - (slope * dist)[None]
            m_old = m_ref[...].reshape(2, tq, 1)
            m_new = jnp.maximum(m_old, jnp.max(s, axis=-1, keepdims=True))
            p = jnp.exp(s - m_new)
            w_old = jnp.exp(m_old - m_new)
            l_ref[...] = (l_ref[...].reshape(2, tq, 1) * w_old
                          + jnp.sum(p, axis=-1, keepdims=True)).reshape(2 * tq, 1)
            pv = jnp.dot(p.reshape(2 * tq, tk).astype(BF16), vt, preferred_element_type=F32)
            acc_ref[...] = acc_ref[...] * w_old.reshape(2 * tq, 1) + pv
            m_ref[...] = m_new.reshape(2 * tq, 1)
            return carry

        lax.fori_loop(0, s_s // tk, body, 0)

    tile(klo_ref, vlo_ref, 0)
    if s_p > s_s:
        @pl.when(length > s_s)
        def _():
            tile(khi_ref, vhi_ref, s_s)

    lam = (jnp.exp(jnp.sum(lq1_ref[...] * lk1_ref[...], axis=-1, keepdims=True))
           - jnp.exp(jnp.sum(lq2_ref[...] * lk2_ref[...], axis=-1, keepdims=True)) + lam_init)
    o = acc_ref[...] / l_ref[...]
    o = o[0:tq] - lam * o[tq:2 * tq]
    o_ref[...] = (_rms_norm(o, g_ref[...]) * (1.0 - lam_init)).astype(o_ref.dtype)


def _diff_attention(h, lq1, lk1, lq2, lk2, g, layer, lam_init, tq, tk, seqs):
    t = h.shape[0]
    n_p, s_p, s_s = seqs
    assert s_p in (s_s, 2 * s_s), "prompt keys are held as at most two blocks of s_s rows"
    hd = HEAD_DIM
    gq, gk, gv = 3 * N_HEADS_B, 4 * N_HEADS_B, 5 * N_HEADS_B
    tiles_per_blk = s_s // tq
    p_blks = n_p * s_p // s_s

    def lo_blk(i):
        b = i // tiles_per_blk
        return jnp.where(b < p_blks, (b // (s_p // s_s)) * (s_p // s_s), b)

    def hi_blk(i):
        b = i // tiles_per_blk
        return jnp.where(b < p_blks, (b // (s_p // s_s)) * (s_p // s_s) + (s_p // s_s - 1), b)

    slopes = jnp.broadcast_to(jnp.asarray(SLOPES_B, F32)[:, None], (N_HEADS_B, hd))
    lam_spec = pl.BlockSpec((None, 1, DQ_B), lambda hh, i: (layer, 0, 0))
    est = 2 * 4 * s_s * hd * 2 + 4 * tq * hd * 2 + 2 * tq * hd * 10 + 8 * 2 * tq * tk * 4
    return pl.pallas_call(
        functools.partial(_diff_body, tq=tq, tk=tk, seqs=seqs, lam_init=lam_init),
        grid=(N_HEADS_B, t // tq),
        in_specs=[
            pl.BlockSpec((tq, hd), lambda hh, i: (i, gq + hh)),
            pl.BlockSpec((s_s, hd), lambda hh, i: (lo_blk(i), gk + hh)),
            pl.BlockSpec((s_s, hd), lambda hh, i: (hi_blk(i), gk + hh)),
            pl.BlockSpec((s_s, hd), lambda hh, i: (lo_blk(i), gv + hh)),
            pl.BlockSpec((s_s, hd), lambda hh, i: (hi_blk(i), gv + hh)),
            pl.BlockSpec((N_HEADS_B, hd), lambda hh, i: (0, 0)),
            lam_spec, lam_spec, lam_spec, lam_spec,
            pl.BlockSpec((None, 1, hd), lambda hh, i: (layer, 0, 0)),
        ],
        out_specs=pl.BlockSpec((tq, hd), lambda hh, i: (i, hh)),
        out_shape=jax.ShapeDtypeStruct((t, N_HEADS_B * hd), BF16),
        scratch_shapes=[pltpu.VMEM((2 * tq, hd), BF16), pltpu.VMEM((2 * tq, 1), F32),
                        pltpu.VMEM((2 * tq, 1), F32), pltpu.VMEM((2 * tq, hd), F32)],
        compiler_params=pltpu.CompilerParams(
            dimension_semantics=("parallel", "parallel"), vmem_limit_bytes=_vmem_limit(est)),
        name="diff_attn",
    )(h, h, h, h, h, slopes, lq1, lk1, lq2, lk2, g)


V7X_VMEM_BYTES = 64 * MIB
V7X_VMEM_CEILING = V7X_VMEM_BYTES - 6 * MIB


def _vmem_limit(estimate_bytes):
    return int(min(max(2 * estimate_bytes, 32 * MIB), V7X_VMEM_CEILING))


def _largest_tile(n, cap, step):
    best = None
    for c in range(step, min(n, cap) + 1, step):
        if n % c == 0:
            best = c
    assert best is not None, (n, cap, step)
    return best


def _plan(t, d_ff, s_p, s_s):
    max_dil = max(dil for _, dil in DILATED_CONFIGS)
    seq_unit = s_s if s_p % s_s == 0 else 1
    return dict(
        tm=_largest_tile(t, 512, 128),
        tf=_largest_tile(d_ff, 512, 128),
        tq_dil=_largest_tile(seq_unit // max_dil, 256, 64),
        tq_diff=_largest_tile(s_s, 256, 128),
        tk_diff=_largest_tile(s_s, 512, 128),
    )


def kernel(x_prompt, x_sample, ln1_g, ln1_b, ffn1_gate, ffn1_up, ffn1_down, w_in, norm_a_g, lam_q1, lam_k1,
           lam_q2, lam_k2, subln_g, w_out, ln2_g, ln2_b, ffn2_gate, ffn2_up, ffn2_down, ln3_g, ln3_b):
    n_p, s_p, d = x_prompt.shape
    n_s, s_s, _ = x_sample.shape
    depth = w_in.shape[0]
    assert w_in.shape[-1] == N_GROUPS * GROUP_WIDTH
    seqs = (n_p, s_p, s_s)
    p_tot = n_p * s_p
    x = jnp.concatenate([x_prompt.reshape(p_tot, d), x_sample.reshape(n_s * s_s, d)], axis=0)
    t = x.shape[0]
    plan = _plan(t, ffn1_gate.shape[-1], s_p, s_s)
    alpha = (2 * depth) ** 0.25

    row = lambda v: v.reshape(depth, 1, v.shape[-1])
    bf = lambda v: v.astype(BF16)
    ffn1 = (bf(ffn1_gate), bf(ffn1_up), bf(ffn1_down), row(ln1_g), row(ln1_b))
    ffn2 = (bf(ffn2_gate), bf(ffn2_up), bf(ffn2_down), row(ln3_g), row(ln3_b))
    w_in_b, w_out_b = bf(w_in), bf(w_out)
    ln2 = (row(ln2_g), row(ln2_b))
    ga, gb = row(norm_a_g), row(subln_g)
    lams = [row(v) for v in (lam_q1, lam_k1, lam_q2, lam_k2)]

    for i in range(depth):
        x = _ffn(x, *ffn1, i, alpha, plan["tm"], plan["tf"])
        h = _in_proj(x, w_in_b, i, plan["tm"])
        ya = _dilated_attention(h, ga, i, plan["tq_dil"], seqs)
        lam_init = 0.8 - 0.6 * math.exp(-0.3 * i)
        yb = _diff_attention(h, *lams, gb, i, lam_init, plan["tq_diff"], plan["tk_diff"], seqs)
        x = _out_proj(x, ya, yb, w_out_b, *ln2, i, alpha, plan["tm"])
        x = _ffn(x, *ffn2, i, alpha, plan["tm"], plan["tf"])
    return x[:p_tot].reshape(x_prompt.shape), x[p_tot:].reshape(x_sample.shape)
```

```python
import functools
import math

import jax
import jax.numpy as jnp
from jax import lax
from jax.experimental import pallas as pl
from jax.experimental.pallas import tpu as pltpu

F32 = jnp.float32
BF16 = jnp.bfloat16

HEAD_DIM = 128
N_HEADS_A = 8
N_HEADS_B = 8
DQ_B = HEAD_DIM // 2
GROUP_WIDTH = N_HEADS_A * HEAD_DIM
N_GROUPS = 6
DILATED_CONFIGS = ((128, 1), (512, 4), (2048, 16))
ALIBI_MAX_BIAS = 8.0
LN_EPS = 1e-5
RMS_EPS = 1e-6
NEG_INF = -1e30

LANES = 128
MIB = 1024 * 1024
NT_DIMS = (((1,), (1,)), ((), ()))
LOG2E = math.log2(math.e)
Q_SCALE_A = HEAD_DIM ** -0.5 * LOG2E
Q_SCALE_B = DQ_B ** -0.5 * LOG2E


def _alibi_slope(n):
    return 2.0 ** (-ALIBI_MAX_BIAS * n / (N_HEADS_A + N_HEADS_B))


SLOPES_A = tuple(_alibi_slope(2 * h + 1) for h in range(N_HEADS_A))
SLOPES_B = tuple(_alibi_slope(2 * h + 2) for h in range(N_HEADS_B))


def _layer_norm(y, g, b):
    mu = jnp.mean(y, axis=-1, keepdims=True)
    d = y - mu
    var = jnp.mean(d * d, axis=-1, keepdims=True)
    return d * lax.rsqrt(var + LN_EPS) * g + b


def _ffn_body(x_ref, wg_ref, wu_ref, wd_ref, g_ref, b_ref, o_ref, xb_ref, acc_ref, *, alpha):
    j = pl.program_id(1)

    @pl.when(j == 0)
    def _():
        xb_ref[...] = x_ref[...].astype(BF16)
        acc_ref[...] = jnp.zeros_like(acc_ref)

    xb = xb_ref[...]
    gate = jnp.dot(xb, wg_ref[...], preferred_element_type=F32)
    up = jnp.dot(xb, wu_ref[...], preferred_element_type=F32)
    act = (gate * jax.nn.sigmoid(gate) * up).astype(BF16)
    acc_ref[...] += jnp.dot(act, wd_ref[...], preferred_element_type=F32)

    @pl.when(j == pl.num_programs(1) - 1)
    def _():
        y = alpha * x_ref[...] + 0.5 * acc_ref[...]
        o_ref[...] = _layer_norm(y, g_ref[...], b_ref[...])


def _ffn(x, wg, wu, wd, g, b, layer, alpha, tm, tf):
    t, d = x.shape
    f = wg.shape[-1]
    est = 2 * 2 * tm * d * 4 + 2 * 3 * d * tf * 2 + tm * d * 6 + 4 * tm * tf * 4
    return pl.pallas_call(
        functools.partial(_ffn_body, alpha=alpha),
        grid=(t // tm, f // tf),
        in_specs=[
            pl.BlockSpec((tm, d), lambda i, j: (i, 0)),
            pl.BlockSpec((None, d, tf), lambda i, j: (layer, 0, j)),
            pl.BlockSpec((None, d, tf), lambda i, j: (layer, 0, j)),
            pl.BlockSpec((None, tf, d), lambda i, j: (layer, j, 0)),
            pl.BlockSpec((None, 1, d), lambda i, j: (layer, 0, 0)),
            pl.BlockSpec((None, 1, d), lambda i, j: (layer, 0, 0)),
        ],
        out_specs=pl.BlockSpec((tm, d), lambda i, j: (i, 0)),
        out_shape=jax.ShapeDtypeStruct((t, d), F32),
        scratch_shapes=[pltpu.VMEM((tm, d), BF16), pltpu.VMEM((tm, d), F32)],
        compiler_params=pltpu.CompilerParams(
            dimension_semantics=("parallel", "arbitrary"), vmem_limit_bytes=_vmem_limit(est)),
        name="ffn",
    )(x, wg, wu, wd, g, b)


def _in_proj_body(x_ref, w_ref, o_ref, xb_ref):
    j = pl.program_id(1)

    @pl.when(j == 0)
    def _():
        xb_ref[...] = x_ref[...].astype(BF16)

    scale = jnp.where(j == 0, Q_SCALE_A, jnp.where(j == N_GROUPS // 2, Q_SCALE_B, 1.0)).astype(F32)
    o_ref[...] = (jnp.dot(xb_ref[...], w_ref[...], preferred_element_type=F32) * scale).astype(BF16)


def _in_proj(x, w, layer, tm):
    t, d = x.shape
    n = w.shape[-1]
    tn = GROUP_WIDTH
    est = 2 * tm * d * 4 + 2 * d * tn * 2 + 2 * tm * tn * 2 + tm * d * 2 + tm * tn * 4
    return pl.pallas_call(
        _in_proj_body,
        grid=(t // tm, n // tn),
        in_specs=[
            pl.BlockSpec((tm, d), lambda i, j: (i, 0)),
            pl.BlockSpec((None, d, tn), lambda i, j: (layer, 0, j)),
        ],
        out_specs=pl.BlockSpec((tm, tn), lambda i, j: (i, j)),
        out_shape=jax.ShapeDtypeStruct((t, n), BF16),
        scratch_shapes=[pltpu.VMEM((tm, d), BF16)],
        compiler_params=pltpu.CompilerParams(
            dimension_semantics=("parallel", "arbitrary"), vmem_limit_bytes=_vmem_limit(est)),
        name="in_proj",
    )(x, w)


def _out_proj_body(x_ref, ya_ref, yb_ref, w_ref, g_ref, b_ref, o_ref, *, alpha):
    wa = ya_ref.shape[-1]
    y = jnp.dot(ya_ref[...], w_ref[:wa, :], preferred_element_type=F32)
    y = y + jnp.dot(yb_ref[...], w_ref[wa:, :], preferred_element_type=F32)
    o_ref[...] = _layer_norm(alpha * x_ref[...] + y, g_ref[...], b_ref[...])


def _out_proj(x, ya, yb, w, g, b, layer, alpha, tm):
    t, d = x.shape
    wa, wb = ya.shape[-1], yb.shape[-1]
    est = 2 * 2 * tm * d * 4 + 2 * tm * (wa + wb) * 2 + 2 * (wa + wb) * d * 2 + 2 * tm * d * 4
    return pl.pallas_call(
        functools.partial(_out_proj_body, alpha=alpha),
        grid=(t // tm,),
        in_specs=[
            pl.BlockSpec((tm, d), lambda i: (i, 0)),
            pl.BlockSpec((tm, wa), lambda i: (i, 0)),
            pl.BlockSpec((tm, wb), lambda i: (i, 0)),
            pl.BlockSpec((None, wa + wb, d), lambda i: (layer, 0, 0)),
            pl.BlockSpec((None, 1, d), lambda i: (layer, 0, 0)),
            pl.BlockSpec((None, 1, d), lambda i: (layer, 0, 0)),
        ],
        out_specs=pl.BlockSpec((tm, d), lambda i: (i, 0)),
        out_shape=jax.ShapeDtypeStruct((t, d), F32),
        compiler_params=pltpu.CompilerParams(
            dimension_semantics=("parallel",), vmem_limit_bytes=_vmem_limit(est)),
        name="out_proj",
    )(x, ya, yb, w, g, b)


def _seq_bounds(t0, seqs):
    n_p, s_p, s_s = seqs
    p_tot = n_p * s_p
    in_prompt = t0 < p_tot
    lo_p = (t0 // s_p) * s_p
    lo_s = p_tot + ((jnp.maximum(t0, p_tot) - p_tot) // s_s) * s_s
    return jnp.where(in_prompt, lo_p, lo_s), jnp.where(in_prompt, s_p, s_s)


def _kv_block_maps(tq, seqs):
    n_p, s_p, s_s = seqs
    assert s_p in (s_s, 2 * s_s), "a sequence is held as at most two blocks of s_s rows"
    per_seq = s_p // s_s
    tiles_per_blk = s_s // tq
    p_blks = n_p * per_seq

    def lo_blk(i):
        b = i // tiles_per_blk
        return jnp.where(b < p_blks, (b // per_seq) * per_seq, b)

    def hi_blk(i):
        b = i // tiles_per_blk
        return jnp.where(b < p_blks, (b // per_seq) * per_seq + (per_seq - 1), b)

    return lo_blk, hi_blk


ONES_ROWS = 16


def _load_sequence(length, klo_ref, khi_ref, vlo_ref, vhi_ref, kseq_ref, vt_ref, s_s, two_blocks):
    def fill(k_ref, v_ref, base):
        kseq_ref[base:base + s_s, :] = k_ref[...]
        vt_ref[HEAD_DIM:, base:base + s_s] = jnp.ones((ONES_ROWS, s_s), vt_ref.dtype)
        for c in range(s_s // LANES):
            rows = slice(c * LANES, (c + 1) * LANES)
            vt_ref[:HEAD_DIM, base + c * LANES:base + (c + 1) * LANES] = v_ref[rows, :].T

    fill(klo_ref, vlo_ref, 0)
    if two_blocks:
        @pl.when(length > s_s)
        def _():
            fill(khi_ref, vhi_ref, s_s)


def _finish_head(ot, gt_ref, gain, o_ref):
    ms = jnp.mean(ot * ot, axis=0, keepdims=True)
    yt = ot * lax.rsqrt(ms + RMS_EPS) * gt_ref[...]
    if gain != 1.0:
        yt = yt * gain
    o_ref[...] = yt.T.astype(o_ref.dtype)


def _dilated_windows(tq):
    out = []
    for window, dil in DILATED_CONFIGS:
        reach = ((window // 2) // dil) * dil
        halo = -(-reach // LANES) * LANES
        out.append((dil, reach, halo, tq + 2 * halo))
    return out


def _dilated_body(q_ref, klo_ref, khi_ref, vlo_ref, vhi_ref, slope_ref, gt_ref, o_ref, kseq_ref, vt_ref,
                  *bias_refs, tq, seqs):
    n_p, s_p, s_s = seqs
    h = pl.program_id(0)
    i = pl.program_id(1)
    t0 = i * tq
    lo, length = _seq_bounds(t0, seqs)
    qpos0 = t0 - lo
    windows = _dilated_windows(tq)

    @pl.when(i == 0)
    def _():
        slope = slope_ref[pl.ds(h, 1), :][:, :1] * LOG2E
        for (dil, reach, halo, width), bias_ref in zip(windows, bias_refs):
            shape = bias_ref.shape
            d = (lax.broadcasted_iota(jnp.int32, shape, 0) - lax.broadcasted_iota(jnp.int32, shape, 1)
                 - 2 * halo)
            dist = jnp.abs(d.astype(F32))
            valid = dist <= float(reach)
            if dil > 1:
                valid = valid & ((d & (dil - 1)) == 0)
            bias_ref[...] = jnp.where(valid, -slope * dist, NEG_INF)

    @pl.when(qpos0 == 0)
    def _():
        _load_sequence(length, klo_ref, khi_ref, vlo_ref, vhi_ref, kseq_ref, vt_ref, s_s, s_p > s_s)

    q = q_ref[...]
    m_run = acc = None
    for (dil, reach, halo, width), bias_ref in zip(windows, bias_refs):
        w0 = pl.multiple_of(jnp.clip(qpos0 - halo, 0, length - width), LANES)
        shift = pl.multiple_of(w0 - qpos0 + 2 * halo, LANES)
        st = lax.dot_general(kseq_ref[pl.ds(w0, width), :], q, NT_DIMS, preferred_element_type=F32)
        st = st + bias_ref[pl.ds(shift, width), :]
        m_b = jnp.max(st, axis=0, keepdims=True)
        m_new = m_b if m_run is None else jnp.maximum(m_run, m_b)
        p = jnp.exp2(st - m_new).astype(BF16)
        a_b = jnp.dot(vt_ref[:, pl.ds(w0, width)], p, preferred_element_type=F32)
        acc = a_b if m_run is None else acc * jnp.exp2(m_run - m_new) + a_b
        m_run = m_new
    _finish_head(acc[:HEAD_DIM] / acc[HEAD_DIM:HEAD_DIM + 1], gt_ref, 1.0, o_ref)


def _dilated_attention(h, gt, layer, tq, seqs):
    t = h.shape[0]
    n_p, s_p, s_s = seqs
    hd = HEAD_DIM
    for dil, _, _, width in _dilated_windows(tq):
        assert dil & (dil - 1) == 0 and width <= s_s
    lo_blk, hi_blk = _kv_block_maps(tq, seqs)
    gq, gk, gv = 0, N_HEADS_A, 2 * N_HEADS_A
    slopes = jnp.broadcast_to(jnp.asarray(SLOPES_A, F32)[:, None], (N_HEADS_A, hd))
    w_max = max(w for _, _, _, w in _dilated_windows(tq))
    bias_rows = [width + 2 * halo for _, _, halo, width in _dilated_windows(tq)]
    est = 2 * 4 * s_s * hd * 2 + 2 * s_p * hd * 2 + 6 * w_max * tq * 4 + sum(bias_rows) * tq * 4
    return pl.pallas_call(
        functools.partial(_dilated_body, tq=tq, seqs=seqs),
        grid=(N_HEADS_A, t // tq),
        in_specs=[
            pl.BlockSpec((tq, hd), lambda hh, i: (i, gq + hh)),
            pl.BlockSpec((s_s, hd), lambda hh, i: (lo_blk(i), gk + hh)),
            pl.BlockSpec((s_s, hd), lambda hh, i: (hi_blk(i), gk + hh)),
            pl.BlockSpec((s_s, hd), lambda hh, i: (lo_blk(i), gv + hh)),
            pl.BlockSpec((s_s, hd), lambda hh, i: (hi_blk(i), gv + hh)),
            pl.BlockSpec((N_HEADS_A, hd), lambda hh, i: (0, 0)),
            pl.BlockSpec((None, hd, 1), lambda hh, i: (layer, 0, 0)),
        ],
        out_specs=pl.BlockSpec((tq, hd), lambda hh, i: (i, hh)),
        out_shape=jax.ShapeDtypeStruct((t, N_HEADS_A * hd), BF16),
        scratch_shapes=[pltpu.VMEM((s_p, hd), BF16), pltpu.VMEM((hd + ONES_ROWS, s_p), BF16)]
        + [pltpu.VMEM((rows, tq), F32) for rows in bias_rows],
        compiler_params=pltpu.CompilerParams(
            dimension_semantics=("arbitrary", "arbitrary"), vmem_limit_bytes=_vmem_limit(est)),
        name="dilated_attn",
    )(h, h, h, h, h, slopes, gt)


def _diff_body(q_ref, klo_ref, khi_ref, vlo_ref, vhi_ref, slope_ref, lq1_ref, lk1_ref, lq2_ref, lk2_ref,
               gt_ref, o_ref, qbd_ref, kseq_ref, vt_ref, es_ref, s_ref, acc_ref,
               *, tq, tk, seqs, lam_init):
    n_p, s_p, s_s = seqs
    h = pl.program_id(0)
    i = pl.program_id(1)
    t0 = i * tq
    lo, length = _seq_bounds(t0, seqs)
    qpos0 = t0 - lo
    slope = slope_ref[pl.ds(h, 1), :][:, :1] * LOG2E

    @pl.when(qpos0 == 0)
    def _():
        _load_sequence(length, klo_ref, khi_ref, vlo_ref, vhi_ref, kseq_ref, vt_ref, s_s, s_p > s_s)

    q = q_ref[...]
    lane = lax.broadcasted_iota(jnp.int32, q.shape, 1)
    zero = jnp.zeros_like(q)
    qbd_ref[0:tq, :] = jnp.where(lane < DQ_B, q, zero)
    qbd_ref[tq:2 * tq, :] = jnp.where(lane >= DQ_B, q, zero)
    es_ref[...] = slope * (lax.broadcasted_iota(jnp.int32, (tk, tq), 0)
                           - lax.broadcasted_iota(jnp.int32, (tk, tq), 1)).astype(F32)
    acc_ref[...] = jnp.zeros_like(acc_ref)

    def score(j, slot):
        start = pl.multiple_of(j * tk, tk)
        st = lax.dot_general(kseq_ref[pl.ds(start, tk), :], qbd_ref[...], NT_DIMS,
                             preferred_element_type=F32)
        bias = jnp.abs(es_ref[...] + slope * (start - qpos0).astype(F32))
        st = st - jnp.concatenate([bias, bias], axis=1)
        s_ref[slot] = st
        return jnp.max(st, axis=0, keepdims=True)

    def accumulate(j, slot, m_old, m_tile):
        m_new = jnp.maximum(m_old, m_tile)
        p = jnp.exp2(s_ref[slot] - m_new).astype(BF16)
        vt = vt_ref[:, pl.ds(pl.multiple_of(j * tk, tk), tk)]
        acc_ref[...] = (acc_ref[...] * jnp.exp2(m_old - m_new)
                        + jnp.dot(vt, p, preferred_element_type=F32))
        return m_new

    def pair(jj, carry):
        m_run, m_even = carry
        j = 2 * jj
        m_odd = score(j + 1, 1)
        m_run = accumulate(j, 0, m_run, m_even)
        m_even = score(j + 2, 0)
        m_run = accumulate(j + 1, 1, m_run, m_odd)
        return m_run, m_even

    n_pairs = length // (2 * tk)
    m_run = jnp.full((1, 2 * tq), NEG_INF, F32)
    m_run, m_even = lax.fori_loop(0, n_pairs - 1, pair, (m_run, score(0, 0)))
    j_last = 2 * (n_pairs - 1)
    m_odd = score(j_last + 1, 1)
    m_run = accumulate(j_last, 0, m_run, m_even)
    accumulate(j_last + 1, 1, m_run, m_odd)

    lam = (jnp.exp(jnp.sum(lq1_ref[...] * lk1_ref[...], axis=-1, keepdims=True))
           - jnp.exp(jnp.sum(lq2_ref[...] * lk2_ref[...], axis=-1, keepdims=True)) + lam_init)
    ot = acc_ref[:HEAD_DIM, :] / acc_ref[HEAD_DIM:HEAD_DIM + 1, :]
    _finish_head(ot[:, :tq] - lam * ot[:, tq:], gt_ref, 1.0 - lam_init, o_ref)


def _diff_attention(h, lq1, lk1, lq2, lk2, gt, layer, lam_init, tq, tk, seqs):
    t = h.shape[0]
    n_p, s_p, s_s = seqs
    hd = HEAD_DIM
    lo_blk, hi_blk = _kv_block_maps(tq, seqs)
    gq, gk, gv = 3 * N_HEADS_B, 4 * N_HEADS_B, 5 * N_HEADS_B
    slopes = jnp.broadcast_to(jnp.asarray(SLOPES_B, F32)[:, None], (N_HEADS_B, hd))
    lam_spec = pl.BlockSpec((None, 1, DQ_B), lambda hh, i: (layer, 0, 0))
    assert s_s % (2 * tk) == 0, "key tiles are processed in pairs"
    est = 2 * 4 * s_s * hd * 2 + 2 * s_p * hd * 2 + tk * tq * 4 + 8 * 2 * tq * tk * 4
    return pl.pallas_call(
        functools.partial(_diff_body, tq=tq, tk=tk, seqs=seqs, lam_init=lam_init),
        grid=(N_HEADS_B, t // tq),
        in_specs=[
            pl.BlockSpec((tq, hd), lambda hh, i: (i, gq + hh)),
            pl.BlockSpec((s_s, hd), lambda hh, i: (lo_blk(i), gk + hh)),
            pl.BlockSpec((s_s, hd), lambda hh, i: (hi_blk(i), gk + hh)),
            pl.BlockSpec((s_s, hd), lambda hh, i: (lo_blk(i), gv + hh)),
            pl.BlockSpec((s_s, hd), lambda hh, i: (hi_blk(i), gv + hh)),
            pl.BlockSpec((N_HEADS_B, hd), lambda hh, i: (0, 0)),
            lam_spec, lam_spec, lam_spec, lam_spec,
            pl.BlockSpec((None, hd, 1), lambda hh, i: (layer, 0, 0)),
        ],
        out_specs=pl.BlockSpec((tq, hd), lambda hh, i: (i, hh)),
        out_shape=jax.ShapeDtypeStruct((t, N_HEADS_B * hd), BF16),
        scratch_shapes=[pltpu.VMEM((2 * tq, hd), BF16), pltpu.VMEM((s_p, hd), BF16),
                        pltpu.VMEM((hd + ONES_ROWS, s_p), BF16), pltpu.VMEM((tk, tq), F32),
                        pltpu.VMEM((2, tk, 2 * tq), F32), pltpu.VMEM((hd + ONES_ROWS, 2 * tq), F32)],
        compiler_params=pltpu.CompilerParams(
            dimension_semantics=("parallel", "arbitrary"), vmem_limit_bytes=_vmem_limit(est)),
        name="diff_attn",
    )(h, h, h, h, h, slopes, lq1, lk1, lq2, lk2, gt)


V7X_VMEM_BYTES = 64 * MIB
V7X_VMEM_CEILING = V7X_VMEM_BYTES - 6 * MIB


def _vmem_limit(estimate_bytes):
    return int(min(max(2 * estimate_bytes, 32 * MIB), V7X_VMEM_CEILING))


def _largest_tile(n, cap, step):
    best = None
    for c in range(step, min(n, cap) + 1, step):
        if n % c == 0:
            best = c
    assert best is not None, (n, cap, step)
    return best


def _plan(t, d_ff, s_s):
    return dict(
        tm=_largest_tile(t, 512, 128),
        tf=_largest_tile(d_ff, 512, 128),
        tq_dil=_largest_tile(s_s, 512, 128),
        tq_attn=_largest_tile(s_s, 256, 128),
        tk_diff=_largest_tile(s_s, 512, 128),
    )


def kernel(x_prompt, x_sample, ln1_g, ln1_b, ffn1_gate, ffn1_up, ffn1_down, w_in, norm_a_g, lam_q1, lam_k1,
           lam_q2, lam_k2, subln_g, w_out, ln2_g, ln2_b, ffn2_gate, ffn2_up, ffn2_down, ln3_g, ln3_b):
    n_p, s_p, d = x_prompt.shape
    n_s, s_s, _ = x_sample.shape
    depth = w_in.shape[0]
    assert w_in.shape[-1] == N_GROUPS * GROUP_WIDTH
    seqs = (n_p, s_p, s_s)
    p_tot = n_p * s_p
    x = jnp.concatenate([x_prompt.reshape(p_tot, d), x_sample.reshape(n_s * s_s, d)], axis=0)
    t = x.shape[0]
    plan = _plan(t, ffn1_gate.shape[-1], s_s)
    alpha = (2 * depth) ** 0.25

    row = lambda v: v.reshape(depth, 1, v.shape[-1])
    col = lambda v: v.reshape(depth, v.shape[-1], 1)
    bf = lambda v: v.astype(BF16)
    ffn1 = (bf(ffn1_gate), bf(ffn1_up), bf(ffn1_down), row(ln1_g), row(ln1_b))
    ffn2 = (bf(ffn2_gate), bf(ffn2_up), bf(ffn2_down), row(ln3_g), row(ln3_b))
    w_in_b, w_out_b = bf(w_in), bf(w_out)
    ln2 = (row(ln2_g), row(ln2_b))
    gat, gbt = col(norm_a_g), col(subln_g)
    lams = [row(v) for v in (lam_q1, lam_k1, lam_q2, lam_k2)]

    for i in range(depth):
        x = _ffn(x, *ffn1, i, alpha, plan["tm"], plan["tf"])
        h = _in_proj(x, w_in_b, i, plan["tm"])
        ya = _dilated_attention(h, gat, i, plan["tq_dil"], seqs)
        lam_init = 0.8 - 0.6 * math.exp(-0.3 * i)
        yb = _diff_attention(h, *lams, gbt, i, lam_init, plan["tq_attn"], plan["tk_diff"], seqs)
        x = _out_proj(x, ya, yb, w_out_b, *ln2, i, alpha, plan["tm"])
        x = _ffn(x, *ffn2, i, alpha, plan["tm"], plan["tf"])
    return x[:p_tot].reshape(x_prompt.shape), x[p_tot:].reshape(x_sample.shape)
```

```python
import functools
import math

import jax
import jax.numpy as jnp
from jax import lax
from jax.experimental import pallas as pl
from jax.experimental.pallas import tpu as pltpu

F32 = jnp.float32
BF16 = jnp.bfloat16

HEAD_DIM = 128
N_HEADS_A = 8
N_HEADS_B = 8
DQ_B = HEAD_DIM // 2
GROUP_WIDTH = N_HEADS_A * HEAD_DIM
N_GROUPS = 6
DILATED_CONFIGS = ((128, 1), (512, 4), (2048, 16))
ALIBI_MAX_BIAS = 8.0
LN_EPS = 1e-5
RMS_EPS = 1e-6
NEG_INF = -1e30

LANES = 128
MIB = 1024 * 1024
NT_DIMS = (((1,), (1,)), ((), ()))
LOG2E = math.log2(math.e)
Q_SCALE_A = HEAD_DIM ** -0.5 * LOG2E
Q_SCALE_B = DQ_B ** -0.5 * LOG2E


def _alibi_slope(n):
    return 2.0 ** (-ALIBI_MAX_BIAS * n / (N_HEADS_A + N_HEADS_B))


SLOPES_A = tuple(_alibi_slope(2 * h + 1) for h in range(N_HEADS_A))
SLOPES_B = tuple(_alibi_slope(2 * h + 2) for h in range(N_HEADS_B))


def _layer_norm(y, g, b):
    mu = jnp.mean(y, axis=-1, keepdims=True)
    d = y - mu
    var = jnp.mean(d * d, axis=-1, keepdims=True)
    return d * lax.rsqrt(var + LN_EPS) * g + b


def _ffn_body(x_ref, wg_ref, wu_ref, wd_ref, g_ref, b_ref, o_ref, xb_ref, acc_ref, *, alpha):
    j = pl.program_id(1)

    @pl.when(j == 0)
    def _():
        xb_ref[...] = x_ref[...].astype(BF16)
        acc_ref[...] = jnp.zeros_like(acc_ref)

    xb = xb_ref[...]
    gate = jnp.dot(xb, wg_ref[...], preferred_element_type=F32)
    up = jnp.dot(xb, wu_ref[...], preferred_element_type=F32)
    act = (gate * jax.nn.sigmoid(gate) * up).astype(BF16)
    acc_ref[...] += jnp.dot(act, wd_ref[...], preferred_element_type=F32)

    @pl.when(j == pl.num_programs(1) - 1)
    def _():
        y = alpha * x_ref[...] + 0.5 * acc_ref[...]
        o_ref[...] = _layer_norm(y, g_ref[...], b_ref[...])


def _ffn(x, wg, wu, wd, g, b, layer, alpha, tm, tf):
    t, d = x.shape
    f = wg.shape[-1]
    est = 2 * 2 * tm * d * 4 + 2 * 3 * d * tf * 2 + tm * d * 6 + 4 * tm * tf * 4
    return pl.pallas_call(
        functools.partial(_ffn_body, alpha=alpha),
        grid=(t // tm, f // tf),
        in_specs=[
            pl.BlockSpec((tm, d), lambda i, j: (i, 0)),
            pl.BlockSpec((None, d, tf), lambda i, j: (layer, 0, j)),
            pl.BlockSpec((None, d, tf), lambda i, j: (layer, 0, j)),
            pl.BlockSpec((None, tf, d), lambda i, j: (layer, j, 0)),
            pl.BlockSpec((None, 1, d), lambda i, j: (layer, 0, 0)),
            pl.BlockSpec((None, 1, d), lambda i, j: (layer, 0, 0)),
        ],
        out_specs=pl.BlockSpec((tm, d), lambda i, j: (i, 0)),
        out_shape=jax.ShapeDtypeStruct((t, d), F32),
        scratch_shapes=[pltpu.VMEM((tm, d), BF16), pltpu.VMEM((tm, d), F32)],
        compiler_params=pltpu.CompilerParams(
            dimension_semantics=("parallel", "arbitrary"), vmem_limit_bytes=_vmem_limit(est)),
        name="ffn",
    )(x, wg, wu, wd, g, b)


def _in_proj_body(x_ref, w_ref, o_ref, xb_ref):
    j = pl.program_id(1)

    @pl.when(j == 0)
    def _():
        xb_ref[...] = x_ref[...].astype(BF16)

    scale = jnp.where(j == 0, Q_SCALE_A, jnp.where(j == N_GROUPS // 2, Q_SCALE_B, 1.0)).astype(F32)
    o_ref[...] = (jnp.dot(xb_ref[...], w_ref[...], preferred_element_type=F32) * scale).astype(BF16)


def _in_proj(x, w, layer, tm):
    t, d = x.shape
    n = w.shape[-1]
    tn = GROUP_WIDTH
    est = 2 * tm * d * 4 + 2 * d * tn * 2 + 2 * tm * tn * 2 + tm * d * 2 + tm * tn * 4
    return pl.pallas_call(
        _in_proj_body,
        grid=(t // tm, n // tn),
        in_specs=[
            pl.BlockSpec((tm, d), lambda i, j: (i, 0)),
            pl.BlockSpec((None, d, tn), lambda i, j: (layer, 0, j)),
        ],
        out_specs=pl.BlockSpec((tm, tn), lambda i, j: (i, j)),
        out_shape=jax.ShapeDtypeStruct((t, n), BF16),
        scratch_shapes=[pltpu.VMEM((tm, d), BF16)],
        compiler_params=pltpu.CompilerParams(
            dimension_semantics=("parallel", "arbitrary"), vmem_limit_bytes=_vmem_limit(est)),
        name="in_proj",
    )(x, w)


def _out_proj_body(x_ref, ya_ref, yb_ref, w_ref, g_ref, b_ref, o_ref, *, alpha):
    wa = ya_ref.shape[-1]
    y = jnp.dot(ya_ref[...], w_ref[:wa, :], preferred_element_type=F32)
    y = y + jnp.dot(yb_ref[...], w_ref[wa:, :], preferred_element_type=F32)
    o_ref[...] = _layer_norm(alpha * x_ref[...] + y, g_ref[...], b_ref[...])


def _out_proj(x, ya, yb, w, g, b, layer, alpha, tm):
    t, d = x.shape
    wa, wb = ya.shape[-1], yb.shape[-1]
    est = 2 * 2 * tm * d * 4 + 2 * tm * (wa + wb) * 2 + 2 * (wa + wb) * d * 2 + 2 * tm * d * 4
    return pl.pallas_call(
        functools.partial(_out_proj_body, alpha=alpha),
        grid=(t // tm,),
        in_specs=[
            pl.BlockSpec((tm, d), lambda i: (i, 0)),
            pl.BlockSpec((tm, wa), lambda i: (i, 0)),
            pl.BlockSpec((tm, wb), lambda i: (i, 0)),
            pl.BlockSpec((None, wa + wb, d), lambda i: (layer, 0, 0)),
            pl.BlockSpec((None, 1, d), lambda i: (layer, 0, 0)),
            pl.BlockSpec((None, 1, d), lambda i: (layer, 0, 0)),
        ],
        out_specs=pl.BlockSpec((tm, d), lambda i: (i, 0)),
        out_shape=jax.ShapeDtypeStruct((t, d), F32),
        compiler_params=pltpu.CompilerParams(
            dimension_semantics=("parallel",), vmem_limit_bytes=_vmem_limit(est)),
        name="out_proj",
    )(x, ya, yb, w, g, b)


def _seq_bounds(t0, seqs):
    n_p, s_p, s_s = seqs
    p_tot = n_p * s_p
    in_prompt = t0 < p_tot
    lo_p = (t0 // s_p) * s_p
    lo_s = p_tot + ((jnp.maximum(t0, p_tot) - p_tot) // s_s) * s_s
    return jnp.where(in_prompt, lo_p, lo_s), jnp.where(in_prompt, s_p, s_s)


def _kv_block_maps(tq, seqs):
    n_p, s_p, s_s = seqs
    assert s_p in (s_s, 2 * s_s), "a sequence is held as at most two blocks of s_s rows"
    per_seq = s_p // s_s
    tiles_per_blk = s_s // tq
    p_blks = n_p * per_seq

    def lo_blk(i):
        b = i // tiles_per_blk
        return jnp.where(b < p_blks, (b // per_seq) * per_seq, b)

    def hi_blk(i):
        b = i // tiles_per_blk
        return jnp.where(b < p_blks, (b // per_seq) * per_seq + (per_seq - 1), b)

    return lo_blk, hi_blk


def _kv_specs(tq, seqs, k_col, v_col):
    s_s = seqs[2]
    lo_blk, hi_blk = _kv_block_maps(tq, seqs)

    def spec(blk, col):
        return pl.BlockSpec((s_s, HEAD_DIM), lambda hh, i: (blk(i), col + hh), pipeline_mode=pl.Buffered(1))

    return [spec(lo_blk, k_col), spec(hi_blk, k_col), spec(lo_blk, v_col), spec(hi_blk, v_col)]


ONES_ROWS = 16


def _load_sequence(length, klo_ref, khi_ref, vlo_ref, vhi_ref, kseq_ref, vt_ref, s_s, two_blocks):
    def fill(k_ref, v_ref, base):
        kseq_ref[base:base + s_s, :] = k_ref[...]
        vt_ref[HEAD_DIM:, base:base + s_s] = jnp.ones((ONES_ROWS, s_s), vt_ref.dtype)
        for c in range(s_s // LANES):
            rows = slice(c * LANES, (c + 1) * LANES)
            vt_ref[:HEAD_DIM, base + c * LANES:base + (c + 1) * LANES] = v_ref[rows, :].T

    fill(klo_ref, vlo_ref, 0)
    if two_blocks:
        @pl.when(length > s_s)
        def _():
            fill(khi_ref, vhi_ref, s_s)


def _finish_head(ot, gt_ref, gain, o_ref):
    ms = jnp.mean(ot * ot, axis=0, keepdims=True)
    yt = ot * lax.rsqrt(ms + RMS_EPS) * gt_ref[...]
    if gain != 1.0:
        yt = yt * gain
    o_ref[...] = yt.T.astype(o_ref.dtype)


def _dilated_windows(tq, tk):
    out = []
    for window, dil in DILATED_CONFIGS:
        reach = ((window // 2) // dil) * dil
        halo = -(-reach // (tk // 2)) * (tk // 2)
        out.append((dil, reach, halo, tq + 2 * halo))
    return out


def _dilated_body(q_ref, klo_ref, khi_ref, vlo_ref, vhi_ref, slope_ref, gt_ref, o_ref, kseq_ref, vt_ref,
                  bias_ref, s_ref, acc_ref, *, tq, tk, seqs):
    n_p, s_p, s_s = seqs
    h = pl.program_id(0)
    i = pl.program_id(1)
    t0 = i * tq
    lo, length = _seq_bounds(t0, seqs)
    qpos0 = t0 - lo
    windows = _dilated_windows(tq, tk)

    bases = []
    base = 0
    for dil, reach, halo, width in windows:
        bases.append(base)
        base += width + 2 * halo

    @pl.when(i == 0)
    def _():
        slope = slope_ref[pl.ds(h, 1), :][:, :1] * LOG2E
        for (dil, reach, halo, width), b0 in zip(windows, bases):
            shape = (width + 2 * halo, tq)
            d = (lax.broadcasted_iota(jnp.int32, shape, 0) - lax.broadcasted_iota(jnp.int32, shape, 1)
                 - 2 * halo)
            dist = jnp.abs(d.astype(F32))
            valid = dist <= float(reach)
            if dil > 1:
                valid = valid & ((d & (dil - 1)) == 0)
            bias_ref[b0:b0 + shape[0], :] = jnp.where(valid, -slope * dist, NEG_INF)

    @pl.when(qpos0 == 0)
    def _():
        _load_sequence(length, klo_ref, khi_ref, vlo_ref, vhi_ref, kseq_ref, vt_ref, s_s, s_p > s_s)

    firsts, key0, bias0 = [], [], []
    n_tiles = 0
    for (dil, reach, halo, width), b0 in zip(windows, bases):
        w0 = jnp.clip(qpos0 - halo, 0, length - width)
        firsts.append(n_tiles)
        key0.append(w0)
        bias0.append(b0 + w0 - qpos0 + 2 * halo)
        n_tiles += width // tk

    def offsets(j):
        k_off, b_off = key0[0] + j * tk, bias0[0] + j * tk
        for first, k0, b0 in zip(firsts[1:], key0[1:], bias0[1:]):
            k_off = jnp.where(j >= first, k0 + (j - first) * tk, k_off)
            b_off = jnp.where(j >= first, b0 + (j - first) * tk, b_off)
        return pl.multiple_of(k_off, tk // 2), pl.multiple_of(b_off, tk // 2)

    q = q_ref[...]
    acc_ref[...] = jnp.zeros_like(acc_ref)

    def score(j, slot):
        k_off, b_off = offsets(j)
        st = lax.dot_general(kseq_ref[pl.ds(k_off, tk), :], q, NT_DIMS, preferred_element_type=F32)
        st = st + bias_ref[pl.ds(b_off, tk), :]
        s_ref[slot] = st
        return jnp.max(st, axis=0, keepdims=True)

    def accumulate(j, slot, m_old, m_tile):
        k_off, _ = offsets(j)
        m_new = jnp.maximum(m_old, m_tile)
        p = jnp.exp2(s_ref[slot] - m_new).astype(BF16)
        acc_ref[...] = (acc_ref[...] * jnp.exp2(m_old - m_new)
                        + jnp.dot(vt_ref[:, pl.ds(k_off, tk)], p, preferred_element_type=F32))
        return m_new

    def pair(jj, carry):
        m_run, m_even = carry
        j = 2 * jj
        m_odd = score(j + 1, 1)
        m_run = accumulate(j, 0, m_run, m_even)
        m_even = score(j + 2, 0)
        m_run = accumulate(j + 1, 1, m_run, m_odd)
        return m_run, m_even

    m_run = jnp.full((1, tq), NEG_INF, F32)
    if n_tiles % 2:
        m_run, m_last = lax.fori_loop(0, n_tiles // 2, pair, (m_run, score(0, 0)))
        accumulate(n_tiles - 1, 0, m_run, m_last)
    else:
        m_run, m_even = lax.fori_loop(0, n_tiles // 2 - 1, pair, (m_run, score(0, 0)))
        m_odd = score(n_tiles - 1, 1)
        m_run = accumulate(n_tiles - 2, 0, m_run, m_even)
        accumulate(n_tiles - 1, 1, m_run, m_odd)
    _finish_head(acc_ref[:HEAD_DIM, :] / acc_ref[HEAD_DIM:HEAD_DIM + 1, :], gt_ref, 1.0, o_ref)


def _dilated_attention(h, gt, layer, tq, tk, seqs):
    t = h.shape[0]
    n_p, s_p, s_s = seqs
    hd = HEAD_DIM
    assert tq % tk == 0
    for dil, _, _, width in _dilated_windows(tq, tk):
        assert dil & (dil - 1) == 0 and width <= s_s and width % tk == 0
    gq, gk, gv = 0, N_HEADS_A, 2 * N_HEADS_A
    slopes = jnp.broadcast_to(jnp.asarray(SLOPES_A, F32)[:, None], (N_HEADS_A, hd))
    bias_rows = sum(width + 2 * halo for _, _, halo, width in _dilated_windows(tq, tk))
    est = 4 * s_s * hd * 2 + 2 * s_p * hd * 2 + (bias_rows + 8 * tk) * tq * 4
    return pl.pallas_call(
        functools.partial(_dilated_body, tq=tq, tk=tk, seqs=seqs),
        grid=(N_HEADS_A, t // tq),
        in_specs=[
            pl.BlockSpec((tq, hd), lambda hh, i: (i, gq + hh)),
            *_kv_specs(tq, seqs, gk, gv),
            pl.BlockSpec((N_HEADS_A, hd), lambda hh, i: (0, 0)),
            pl.BlockSpec((None, hd, 1), lambda hh, i: (layer, 0, 0)),
        ],
        out_specs=pl.BlockSpec((tq, hd), lambda hh, i: (i, hh)),
        out_shape=jax.ShapeDtypeStruct((t, N_HEADS_A * hd), BF16),
        scratch_shapes=[pltpu.VMEM((s_p, hd), BF16), pltpu.VMEM((hd + ONES_ROWS, s_p), BF16),
                        pltpu.VMEM((bias_rows, tq), F32), pltpu.VMEM((2, tk, tq), F32),
                        pltpu.VMEM((hd + ONES_ROWS, tq), F32)],
        compiler_params=pltpu.CompilerParams(
            dimension_semantics=("arbitrary", "arbitrary"), vmem_limit_bytes=_vmem_limit(est)),
        name="dilated_attn",
    )(h, h, h, h, h, slopes, gt)


def _diff_body(q_ref, klo_ref, khi_ref, vlo_ref, vhi_ref, slope_ref, lq1_ref, lk1_ref, lq2_ref, lk2_ref,
               gt_ref, o_ref, qbd_ref, kseq_ref, vt_ref, bias_ref, s_ref, acc_ref,
               *, tq, tk, seqs, lam_init):
    n_p, s_p, s_s = seqs
    h = pl.program_id(0)
    i = pl.program_id(1)
    t0 = i * tq
    lo, length = _seq_bounds(t0, seqs)
    qpos0 = t0 - lo
    origin = s_p - tq

    @pl.when(i == 0)
    def _():
        slope = slope_ref[pl.ds(h, 1), :][:, :1] * LOG2E
        rows = lax.broadcasted_iota(jnp.int32, (tk, tq), 0) - lax.broadcasted_iota(jnp.int32, (tk, tq), 1)

        def fill(c, carry):
            r0 = pl.multiple_of(c * tk, tk)
            bias_ref[pl.ds(r0, tk), :] = -slope * jnp.abs((rows + (r0 - origin)).astype(F32))
            return carry

        lax.fori_loop(0, bias_ref.shape[0] // tk, fill, 0)

    @pl.when(qpos0 == 0)
    def _():
        _load_sequence(length, klo_ref, khi_ref, vlo_ref, vhi_ref, kseq_ref, vt_ref, s_s, s_p > s_s)

    q = q_ref[...]
    lane = lax.broadcasted_iota(jnp.int32, q.shape, 1)
    zero = jnp.zeros_like(q)
    qbd_ref[0:tq, :] = jnp.where(lane < DQ_B, q, zero)
    qbd_ref[tq:2 * tq, :] = jnp.where(lane >= DQ_B, q, zero)
    acc_ref[...] = jnp.zeros_like(acc_ref)

    def score(j, slot):
        start = pl.multiple_of(j * tk, tk)
        st = lax.dot_general(kseq_ref[pl.ds(start, tk), :], qbd_ref[...], NT_DIMS,
                             preferred_element_type=F32)
        bias = bias_ref[pl.ds(pl.multiple_of(start - qpos0 + origin, tq), tk), :]
        st = st + jnp.concatenate([bias, bias], axis=1)
        s_ref[slot] = st
        return jnp.max(st, axis=0, keepdims=True)

    def accumulate(j, slot, m_old, m_tile):
        m_new = jnp.maximum(m_old, m_tile)
        p = jnp.exp2(s_ref[slot] - m_new).astype(BF16)
        vt = vt_ref[:, pl.ds(pl.multiple_of(j * tk, tk), tk)]
        acc_ref[...] = (acc_ref[...] * jnp.exp2(m_old - m_new)
                        + jnp.dot(vt, p, preferred_element_type=F32))
        return m_new

    def pair(jj, carry):
        m_run, m_even = carry
        j = 2 * jj
        m_odd = score(j + 1, 1)
        m_run = accumulate(j, 0, m_run, m_even)
        m_even = score(j + 2, 0)
        m_run = accumulate(j + 1, 1, m_run, m_odd)
        return m_run, m_even

    n_pairs = length // (2 * tk)
    m_run = jnp.full((1, 2 * tq), NEG_INF, F32)
    m_run, m_even = lax.fori_loop(0, n_pairs - 1, pair, (m_run, score(0, 0)))
    j_last = 2 * (n_pairs - 1)
    m_odd = score(j_last + 1, 1)
    m_run = accumulate(j_last, 0, m_run, m_even)
    accumulate(j_last + 1, 1, m_run, m_odd)

    lam = (jnp.exp(jnp.sum(lq1_ref[...] * lk1_ref[...], axis=-1, keepdims=True))
           - jnp.exp(jnp.sum(lq2_ref[...] * lk2_ref[...], axis=-1, keepdims=True)) + lam_init)
    ot = acc_ref[:HEAD_DIM, :] / acc_ref[HEAD_DIM:HEAD_DIM + 1, :]
    _finish_head(ot[:, :tq] - lam * ot[:, tq:], gt_ref, 1.0 - lam_init, o_ref)


def _diff_attention(h, lq1, lk1, lq2, lk2, gt, layer, lam_init, tq, tk, seqs):
    t = h.shape[0]
    n_p, s_p, s_s = seqs
    hd = HEAD_DIM
    gq, gk, gv = 3 * N_HEADS_B, 4 * N_HEADS_B, 5 * N_HEADS_B
    slopes = jnp.broadcast_to(jnp.asarray(SLOPES_B, F32)[:, None], (N_HEADS_B, hd))
    lam_spec = pl.BlockSpec((None, 1, DQ_B), lambda hh, i: (layer, 0, 0))
    assert s_s % (2 * tk) == 0, "key tiles are processed in pairs"
    assert tk % tq == 0 and s_p % tk == 0
    bias_rows = 2 * s_p
    est = 4 * s_s * hd * 2 + 2 * s_p * hd * 2 + bias_rows * tq * 4 + 8 * 2 * tq * tk * 4
    return pl.pallas_call(
        functools.partial(_diff_body, tq=tq, tk=tk, seqs=seqs, lam_init=lam_init),
        grid=(N_HEADS_B, t // tq),
        in_specs=[
            pl.BlockSpec((tq, hd), lambda hh, i: (i, gq + hh)),
            *_kv_specs(tq, seqs, gk, gv),
            pl.BlockSpec((N_HEADS_B, hd), lambda hh, i: (0, 0)),
            lam_spec, lam_spec, lam_spec, lam_spec,
            pl.BlockSpec((None, hd, 1), lambda hh, i: (layer, 0, 0)),
        ],
        out_specs=pl.BlockSpec((tq, hd), lambda hh, i: (i, hh)),
        out_shape=jax.ShapeDtypeStruct((t, N_HEADS_B * hd), BF16),
        scratch_shapes=[pltpu.VMEM((2 * tq, hd), BF16), pltpu.VMEM((s_p, hd), BF16),
                        pltpu.VMEM((hd + ONES_ROWS, s_p), BF16), pltpu.VMEM((bias_rows, tq), F32),
                        pltpu.VMEM((2, tk, 2 * tq), F32), pltpu.VMEM((hd + ONES_ROWS, 2 * tq), F32)],
        compiler_params=pltpu.CompilerParams(
            dimension_semantics=("arbitrary", "arbitrary"), vmem_limit_bytes=_vmem_limit(est)),
        name="diff_attn",
    )(h, h, h, h, h, slopes, lq1, lk1, lq2, lk2, gt)


V7X_VMEM_BYTES = 64 * MIB
V7X_VMEM_CEILING = V7X_VMEM_BYTES - 6 * MIB


def _vmem_limit(estimate_bytes):
    return int(min(max(2 * estimate_bytes, 32 * MIB), V7X_VMEM_CEILING))


def _largest_tile(n, cap, step):
    best = None
    for c in range(step, min(n, cap) + 1, step):
        if n % c == 0:
            best = c
    assert best is not None, (n, cap, step)
    return best


def _plan(t, d_ff, s_s):
    return dict(
        tm=_largest_tile(t, 512, 128),
        tm_proj=_largest_tile(t, 1024, 128),
        tf=_largest_tile(d_ff, 512, 128),
        tq_dil=_largest_tile(s_s, 512, 128),
        tk_dil=_largest_tile(s_s, 512, 128),
        tq_diff=_largest_tile(s_s, 256, 128),
        tk_diff=_largest_tile(s_s, 1024, 128),
    )


def kernel(x_prompt, x_sample, ln1_g, ln1_b, ffn1_gate, ffn1_up, ffn1_down, w_in, norm_a_g, lam_q1, lam_k1,
           lam_q2, lam_k2, subln_g, w_out, ln2_g, ln2_b, ffn2_gate, ffn2_up, ffn2_down, ln3_g, ln3_b):
    n_p, s_p, d = x_prompt.shape
    n_s, s_s, _ = x_sample.shape
    depth = w_in.shape[0]
    assert w_in.shape[-1] == N_GROUPS * GROUP_WIDTH
    seqs = (n_p, s_p, s_s)
    p_tot = n_p * s_p
    x = jnp.concatenate([x_prompt.reshape(p_tot, d), x_sample.reshape(n_s * s_s, d)], axis=0)
    t = x.shape[0]
    plan = _plan(t, ffn1_gate.shape[-1], s_s)
    alpha = (2 * depth) ** 0.25

    row = lambda v: v.reshape(depth, 1, v.shape[-1])
    col = lambda v: v.reshape(depth, v.shape[-1], 1)
    bf = lambda v: v.astype(BF16)
    ffn1 = (bf(ffn1_gate), bf(ffn1_up), bf(ffn1_down), row(ln1_g), row(ln1_b))
    ffn2 = (bf(ffn2_gate), bf(ffn2_up), bf(ffn2_down), row(ln3_g), row(ln3_b))
    w_in_b, w_out_b = bf(w_in), bf(w_out)
    ln2 = (row(ln2_g), row(ln2_b))
    gat, gbt = col(norm_a_g), col(subln_g)
    lams = [row(v) for v in (lam_q1, lam_k1, lam_q2, lam_k2)]

    for i in range(depth):
        x = _ffn(x, *ffn1, i, alpha, plan["tm"], plan["tf"])
        h = _in_proj(x, w_in_b, i, plan["tm_proj"])
        ya = _dilated_attention(h, gat, i, plan["tq_dil"], plan["tk_dil"], seqs)
        lam_init = 0.8 - 0.6 * math.exp(-0.3 * i)
        yb = _diff_attention(h, *lams, gbt, i, lam_init, plan["tq_diff"], plan["tk_diff"], seqs)
        x = _out_proj(x, ya, yb, w_out_b, *ln2, i, alpha, plan["tm"])
        x = _ffn(x, *ffn2, i, alpha, plan["tm"], plan["tf"])
    return x[:p_tot].reshape(x_prompt.shape), x[p_tot:].reshape(x_sample.shape)
```

```python
import functools
import math

import jax
import jax.numpy as jnp
from jax import lax
from jax.experimental import pallas as pl
from jax.experimental.pallas import tpu as pltpu

F32 = jnp.float32
BF16 = jnp.bfloat16

HEAD_DIM = 128
N_HEADS_A = 8
N_HEADS_B = 8
DQ_B = HEAD_DIM // 2
GROUP_WIDTH = N_HEADS_A * HEAD_DIM
N_GROUPS = 6
DILATED_CONFIGS = ((128, 1), (512, 4), (2048, 16))
ALIBI_MAX_BIAS = 8.0
LN_EPS = 1e-5
RMS_EPS = 1e-6
NEG_INF = -1e30

LANES = 128
MIB = 1024 * 1024
NT_DIMS = (((1,), (1,)), ((), ()))
LOG2E = math.log2(math.e)
Q_SCALE_A = HEAD_DIM ** -0.5 * LOG2E
Q_SCALE_B = DQ_B ** -0.5 * LOG2E


def _alibi_slope(n):
    return 2.0 ** (-ALIBI_MAX_BIAS * n / (N_HEADS_A + N_HEADS_B))


SLOPES_A = tuple(_alibi_slope(2 * h + 1) for h in range(N_HEADS_A))
SLOPES_B = tuple(_alibi_slope(2 * h + 2) for h in range(N_HEADS_B))


def _layer_norm(y, g, b):
    mu = jnp.mean(y, axis=-1, keepdims=True)
    d = y - mu
    var = jnp.mean(d * d, axis=-1, keepdims=True)
    return d * lax.rsqrt(var + LN_EPS) * g + b


def _ffn_body(x_ref, wg_ref, wu_ref, wd_ref, g_ref, b_ref, o_ref, xb_ref, acc_ref, *, alpha):
    j = pl.program_id(1)

    @pl.when(j == 0)
    def _():
        xb_ref[...] = x_ref[...].astype(BF16)
        acc_ref[...] = jnp.zeros_like(acc_ref)

    xb = xb_ref[...]
    gate = jnp.dot(xb, wg_ref[...], preferred_element_type=F32)
    up = jnp.dot(xb, wu_ref[...], preferred_element_type=F32)
    act = (gate * jax.nn.sigmoid(gate) * up).astype(BF16)
    acc_ref[...] += jnp.dot(act, wd_ref[...], preferred_element_type=F32)

    @pl.when(j == pl.num_programs(1) - 1)
    def _():
        y = alpha * x_ref[...] + 0.5 * acc_ref[...]
        o_ref[...] = _layer_norm(y, g_ref[...], b_ref[...])


def _ffn(x, wg, wu, wd, g, b, layer, alpha, tm, tf):
    t, d = x.shape
    f = wg.shape[-1]
    est = 2 * 2 * tm * d * 4 + 2 * 3 * d * tf * 2 + tm * d * 6 + 4 * tm * tf * 4
    return pl.pallas_call(
        functools.partial(_ffn_body, alpha=alpha),
        grid=(t // tm, f // tf),
        in_specs=[
            pl.BlockSpec((tm, d), lambda i, j: (i, 0)),
            pl.BlockSpec((None, d, tf), lambda i, j: (layer, 0, j)),
            pl.BlockSpec((None, d, tf), lambda i, j: (layer, 0, j)),
            pl.BlockSpec((None, tf, d), lambda i, j: (layer, j, 0)),
            pl.BlockSpec((None, 1, d), lambda i, j: (layer, 0, 0)),
            pl.BlockSpec((None, 1, d), lambda i, j: (layer, 0, 0)),
        ],
        out_specs=pl.BlockSpec((tm, d), lambda i, j: (i, 0)),
        out_shape=jax.ShapeDtypeStruct((t, d), F32),
        scratch_shapes=[pltpu.VMEM((tm, d), BF16), pltpu.VMEM((tm, d), F32)],
        compiler_params=pltpu.CompilerParams(
            dimension_semantics=("parallel", "arbitrary"), vmem_limit_bytes=_vmem_limit(est)),
        name="ffn",
    )(x, wg, wu, wd, g, b)


def _in_proj_body(x_ref, w_ref, o_ref, xb_ref):
    j = pl.program_id(1)

    @pl.when(j == 0)
    def _():
        xb_ref[...] = x_ref[...].astype(BF16)

    scale = jnp.where(j == 0, Q_SCALE_A, jnp.where(j == N_GROUPS // 2, Q_SCALE_B, 1.0)).astype(F32)
    o_ref[...] = (jnp.dot(xb_ref[...], w_ref[...], preferred_element_type=F32) * scale).astype(BF16)


def _in_proj(x, w, layer, tm):
    t, d = x.shape
    n = w.shape[-1]
    tn = GROUP_WIDTH
    est = 2 * tm * d * 4 + 2 * d * tn * 2 + 2 * tm * tn * 2 + tm * d * 2 + tm * tn * 4
    return pl.pallas_call(
        _in_proj_body,
        grid=(t // tm, n // tn),
        in_specs=[
            pl.BlockSpec((tm, d), lambda i, j: (i, 0)),
            pl.BlockSpec((None, d, tn), lambda i, j: (layer, 0, j)),
        ],
        out_specs=pl.BlockSpec((tm, tn), lambda i, j: (i, j)),
        out_shape=jax.ShapeDtypeStruct((t, n), BF16),
        scratch_shapes=[pltpu.VMEM((tm, d), BF16)],
        compiler_params=pltpu.CompilerParams(
            dimension_semantics=("parallel", "arbitrary"), vmem_limit_bytes=_vmem_limit(est)),
        name="in_proj",
    )(x, w)


def _out_proj_body(x_ref, ya_ref, yb_ref, w_ref, g_ref, b_ref, o_ref, *, alpha):
    wa = ya_ref.shape[-1]
    y = jnp.dot(ya_ref[...], w_ref[:wa, :], preferred_element_type=F32)
    y = y + jnp.dot(yb_ref[...], w_ref[wa:, :], preferred_element_type=F32)
    o_ref[...] = _layer_norm(alpha * x_ref[...] + y, g_ref[...], b_ref[...])


def _out_proj(x, ya, yb, w, g, b, layer, alpha, tm):
    t, d = x.shape
    wa, wb = ya.shape[-1], yb.shape[-1]
    est = 2 * 2 * tm * d * 4 + 2 * tm * (wa + wb) * 2 + 2 * (wa + wb) * d * 2 + 2 * tm * d * 4
    return pl.pallas_call(
        functools.partial(_out_proj_body, alpha=alpha),
        grid=(t // tm,),
        in_specs=[
            pl.BlockSpec((tm, d), lambda i: (i, 0)),
            pl.BlockSpec((tm, wa), lambda i: (i, 0)),
            pl.BlockSpec((tm, wb), lambda i: (i, 0)),
            pl.BlockSpec((None, wa + wb, d), lambda i: (layer, 0, 0)),
            pl.BlockSpec((None, 1, d), lambda i: (layer, 0, 0)),
            pl.BlockSpec((None, 1, d), lambda i: (layer, 0, 0)),
        ],
        out_specs=pl.BlockSpec((tm, d), lambda i: (i, 0)),
        out_shape=jax.ShapeDtypeStruct((t, d), F32),
        compiler_params=pltpu.CompilerParams(
            dimension_semantics=("parallel",), vmem_limit_bytes=_vmem_limit(est)),
        name="out_proj",
    )(x, ya, yb, w, g, b)


def _seq_bounds(t0, seqs):
    n_p, s_p, s_s = seqs
    p_tot = n_p * s_p
    in_prompt = t0 < p_tot
    lo_p = (t0 // s_p) * s_p
    lo_s = p_tot + ((jnp.maximum(t0, p_tot) - p_tot) // s_s) * s_s
    return jnp.where(in_prompt, lo_p, lo_s), jnp.where(in_prompt, s_p, s_s)


def _kv_block_maps(tq, seqs):
    n_p, s_p, s_s = seqs
    assert s_p in (s_s, 2 * s_s), "a sequence is held as at most two blocks of s_s rows"
    per_seq = s_p // s_s
    tiles_per_blk = s_s // tq
    p_blks = n_p * per_seq

    def lo_blk(i):
        b = i // tiles_per_blk
        return jnp.where(b < p_blks, (b // per_seq) * per_seq, b)

    def hi_blk(i):
        b = i // tiles_per_blk
        return jnp.where(b < p_blks, (b // per_seq) * per_seq + (per_seq - 1), b)

    return lo_blk, hi_blk


def _kv_specs(tq, seqs, k_col, v_col):
    s_s = seqs[2]
    lo_blk, hi_blk = _kv_block_maps(tq, seqs)

    def spec(blk, col):
        return pl.BlockSpec((s_s, HEAD_DIM), lambda hh, i: (blk(i), col + hh), pipeline_mode=pl.Buffered(1))

    return [spec(lo_blk, k_col), spec(hi_blk, k_col), spec(lo_blk, v_col), spec(hi_blk, v_col)]


ONES_ROWS = 16
BIAS_LANES = 256


def _load_sequence(length, klo_ref, khi_ref, vlo_ref, vhi_ref, kseq_ref, vt_ref, s_s, two_blocks):
    def fill(k_ref, v_ref, base):
        kseq_ref[base:base + s_s, :] = k_ref[...]
        vt_ref[HEAD_DIM:, base:base + s_s] = jnp.ones((ONES_ROWS, s_s), vt_ref.dtype)
        for c in range(s_s // LANES):
            rows = slice(c * LANES, (c + 1) * LANES)
            vt_ref[:HEAD_DIM, base + c * LANES:base + (c + 1) * LANES] = v_ref[rows, :].T

    fill(klo_ref, vlo_ref, 0)
    if two_blocks:
        @pl.when(length > s_s)
        def _():
            fill(khi_ref, vhi_ref, s_s)


def _finish_head(ot, gt_ref, gain, o_ref):
    ms = jnp.mean(ot * ot, axis=0, keepdims=True)
    yt = ot * lax.rsqrt(ms + RMS_EPS) * gt_ref[...]
    if gain != 1.0:
        yt = yt * gain
    o_ref[...] = yt.T.astype(o_ref.dtype)


def _dilated_windows(tq, tk):
    out = []
    for window, dil in DILATED_CONFIGS:
        reach = ((window // 2) // dil) * dil
        halo = -(-reach // (tk // 2)) * (tk // 2)
        out.append((dil, reach, halo, tq + 2 * halo))
    return out


def _dilated_body(q_ref, klo_ref, khi_ref, vlo_ref, vhi_ref, slope_ref, gt_ref, o_ref, kseq_ref, vt_ref,
                  bias_ref, s_ref, acc_ref, *, tq, tk, seqs):
    n_p, s_p, s_s = seqs
    h = pl.program_id(0)
    i = pl.program_id(1)
    t0 = i * tq
    lo, length = _seq_bounds(t0, seqs)
    qpos0 = t0 - lo
    windows = _dilated_windows(tq, tk)

    bases = []
    base = 0
    for dil, reach, halo, width in windows:
        bases.append(base)
        base += width + 2 * halo

    @pl.when(i == 0)
    def _():
        slope = slope_ref[pl.ds(h, 1), :][:, :1] * LOG2E
        for (dil, reach, halo, width), b0 in zip(windows, bases):
            shape = (width + 2 * halo, tq)
            d = (lax.broadcasted_iota(jnp.int32, shape, 0) - lax.broadcasted_iota(jnp.int32, shape, 1)
                 - 2 * halo)
            dist = jnp.abs(d.astype(F32))
            valid = dist <= float(reach)
            if dil > 1:
                valid = valid & ((d & (dil - 1)) == 0)
            bias_ref[b0:b0 + shape[0], :] = jnp.where(valid, -slope * dist, NEG_INF)

    @pl.when(qpos0 == 0)
    def _():
        _load_sequence(length, klo_ref, khi_ref, vlo_ref, vhi_ref, kseq_ref, vt_ref, s_s, s_p > s_s)

    firsts, key0, bias0 = [], [], []
    n_tiles = 0
    for (dil, reach, halo, width), b0 in zip(windows, bases):
        w0 = jnp.clip(qpos0 - halo, 0, length - width)
        firsts.append(n_tiles)
        key0.append(w0)
        bias0.append(b0 + w0 - qpos0 + 2 * halo)
        n_tiles += width // tk

    def offsets(j):
        k_off, b_off = key0[0] + j * tk, bias0[0] + j * tk
        for first, k0, b0 in zip(firsts[1:], key0[1:], bias0[1:]):
            k_off = jnp.where(j >= first, k0 + (j - first) * tk, k_off)
            b_off = jnp.where(j >= first, b0 + (j - first) * tk, b_off)
        return pl.multiple_of(k_off, tk // 2), pl.multiple_of(b_off, tk // 2)

    q = q_ref[...]
    acc_ref[...] = jnp.zeros_like(acc_ref)

    def score(j, slot):
        k_off, b_off = offsets(j)
        st = lax.dot_general(kseq_ref[pl.ds(k_off, tk), :], q, NT_DIMS, preferred_element_type=F32)
        st = st + bias_ref[pl.ds(b_off, tk), :]
        s_ref[slot] = st
        return jnp.max(st, axis=0, keepdims=True)

    def accumulate(j, slot, m_old, m_tile):
        k_off, _ = offsets(j)
        m_new = jnp.maximum(m_old, m_tile)
        p = jnp.exp2(s_ref[slot] - m_new).astype(BF16)
        acc_ref[...] = (acc_ref[...] * jnp.exp2(m_old - m_new)
                        + jnp.dot(vt_ref[:, pl.ds(k_off, tk)], p, preferred_element_type=F32))
        return m_new

    def pair(jj, carry):
        m_run, m_even = carry
        j = 2 * jj
        m_odd = score(j + 1, 1)
        m_run = accumulate(j, 0, m_run, m_even)
        m_even = score(j + 2, 0)
        m_run = accumulate(j + 1, 1, m_run, m_odd)
        return m_run, m_even

    m_run = jnp.full((1, tq), NEG_INF, F32)
    if n_tiles % 2:
        m_run, m_last = lax.fori_loop(0, n_tiles // 2, pair, (m_run, score(0, 0)))
        accumulate(n_tiles - 1, 0, m_run, m_last)
    else:
        m_run, m_even = lax.fori_loop(0, n_tiles // 2 - 1, pair, (m_run, score(0, 0)))
        m_odd = score(n_tiles - 1, 1)
        m_run = accumulate(n_tiles - 2, 0, m_run, m_even)
        accumulate(n_tiles - 1, 1, m_run, m_odd)
    _finish_head(acc_ref[:HEAD_DIM, :] / acc_ref[HEAD_DIM:HEAD_DIM + 1, :], gt_ref, 1.0, o_ref)


def _dilated_attention(h, gt, layer, tq, tk, seqs):
    t = h.shape[0]
    n_p, s_p, s_s = seqs
    hd = HEAD_DIM
    assert tq % tk == 0
    for dil, _, _, width in _dilated_windows(tq, tk):
        assert dil & (dil - 1) == 0 and width <= s_s and width % tk == 0
    gq, gk, gv = 0, N_HEADS_A, 2 * N_HEADS_A
    slopes = jnp.broadcast_to(jnp.asarray(SLOPES_A, F32)[:, None], (N_HEADS_A, hd))
    bias_rows = sum(width + 2 * halo for _, _, halo, width in _dilated_windows(tq, tk))
    est = 4 * s_s * hd * 2 + 2 * s_p * hd * 2 + (bias_rows + 8 * tk) * tq * 4
    return pl.pallas_call(
        functools.partial(_dilated_body, tq=tq, tk=tk, seqs=seqs),
        grid=(N_HEADS_A, t // tq),
        in_specs=[
            pl.BlockSpec((tq, hd), lambda hh, i: (i, gq + hh)),
            *_kv_specs(tq, seqs, gk, gv),
            pl.BlockSpec((N_HEADS_A, hd), lambda hh, i: (0, 0)),
            pl.BlockSpec((None, hd, 1), lambda hh, i: (layer, 0, 0)),
        ],
        out_specs=pl.BlockSpec((tq, hd), lambda hh, i: (i, hh)),
        out_shape=jax.ShapeDtypeStruct((t, N_HEADS_A * hd), BF16),
        scratch_shapes=[pltpu.VMEM((s_p, hd), BF16), pltpu.VMEM((hd + ONES_ROWS, s_p), BF16),
                        pltpu.VMEM((bias_rows, tq), F32), pltpu.VMEM((2, tk, tq), F32),
                        pltpu.VMEM((hd + ONES_ROWS, tq), F32)],
        compiler_params=pltpu.CompilerParams(
            dimension_semantics=("arbitrary", "arbitrary"), vmem_limit_bytes=_vmem_limit(est)),
        name="dilated_attn",
    )(h, h, h, h, h, slopes, gt)


def _diff_body(q_ref, klo_ref, khi_ref, vlo_ref, vhi_ref, slope_ref, lq1_ref, lk1_ref, lq2_ref, lk2_ref,
               gt_ref, o_ref, qbd_ref, kseq_ref, vt_ref, bias_ref, s_ref, acc_ref,
               *, tq, tk, seqs, lam_init):
    n_p, s_p, s_s = seqs
    h = pl.program_id(0)
    i = pl.program_id(1)
    t0 = i * tq
    lo, length = _seq_bounds(t0, seqs)
    qpos0 = t0 - lo
    tl = bias_ref.shape[1]
    origin = s_p - tl

    @pl.when(i == 0)
    def _():
        slope = slope_ref[pl.ds(h, 1), :][:, :1] * LOG2E
        rows = lax.broadcasted_iota(jnp.int32, (tk, tl), 0) - lax.broadcasted_iota(jnp.int32, (tk, tl), 1)

        def fill(c, carry):
            r0 = pl.multiple_of(c * tk, tk)
            bias_ref[pl.ds(r0, tk), :] = -slope * jnp.abs((rows + (r0 - origin)).astype(F32))
            return carry

        lax.fori_loop(0, bias_ref.shape[0] // tk, fill, 0)

    @pl.when(qpos0 == 0)
    def _():
        _load_sequence(length, klo_ref, khi_ref, vlo_ref, vhi_ref, kseq_ref, vt_ref, s_s, s_p > s_s)

    q = q_ref[...]
    lane = lax.broadcasted_iota(jnp.int32, q.shape, 1)
    zero = jnp.zeros_like(q)
    qbd_ref[0:tq, :] = jnp.where(lane < DQ_B, q, zero)
    qbd_ref[tq:2 * tq, :] = jnp.where(lane >= DQ_B, q, zero)
    acc_ref[...] = jnp.zeros_like(acc_ref)

    def score(j, slot):
        start = pl.multiple_of(j * tk, tk)
        st = lax.dot_general(kseq_ref[pl.ds(start, tk), :], qbd_ref[...], NT_DIMS,
                             preferred_element_type=F32)
        row0 = start - qpos0 + origin
        bias = [bias_ref[pl.ds(pl.multiple_of(row0 - g * tl, tl), tk), :] for g in range(tq // tl)]
        st = st + jnp.concatenate(bias + bias, axis=1)
        s_ref[slot] = st
        return jnp.max(st, axis=0, keepdims=True)

    def accumulate(j, slot, m_old, m_tile):
        m_new = jnp.maximum(m_old, m_tile)
        p = jnp.exp2(s_ref[slot] - m_new).astype(BF16)
        vt = vt_ref[:, pl.ds(pl.multiple_of(j * tk, tk), tk)]
        acc_ref[...] = (acc_ref[...] * jnp.exp2(m_old - m_new)
                        + jnp.dot(vt, p, preferred_element_type=F32))
        return m_new

    def pair(jj, carry):
        m_run, m_even = carry
        j = 2 * jj
        m_odd = score(j + 1, 1)
        m_run = accumulate(j, 0, m_run, m_even)
        m_even = score(j + 2, 0)
        m_run = accumulate(j + 1, 1, m_run, m_odd)
        return m_run, m_even

    n_pairs = length // (2 * tk)
    m_run = jnp.full((1, 2 * tq), NEG_INF, F32)
    m_run, m_even = lax.fori_loop(0, n_pairs - 1, pair, (m_run, score(0, 0)))
    j_last = 2 * (n_pairs - 1)
    m_odd = score(j_last + 1, 1)
    m_run = accumulate(j_last, 0, m_run, m_even)
    accumulate(j_last + 1, 1, m_run, m_odd)

    lam = (jnp.exp(jnp.sum(lq1_ref[...] * lk1_ref[...], axis=-1, keepdims=True))
           - jnp.exp(jnp.sum(lq2_ref[...] * lk2_ref[...], axis=-1, keepdims=True)) + lam_init)
    ot = acc_ref[:HEAD_DIM, :] / acc_ref[HEAD_DIM:HEAD_DIM + 1, :]
    _finish_head(ot[:, :tq] - lam * ot[:, tq:], gt_ref, 1.0 - lam_init, o_ref)


def _diff_attention(h, lq1, lk1, lq2, lk2, gt, layer, lam_init, tq, tk, seqs):
    t = h.shape[0]
    n_p, s_p, s_s = seqs
    hd = HEAD_DIM
    gq, gk, gv = 3 * N_HEADS_B, 4 * N_HEADS_B, 5 * N_HEADS_B
    slopes = jnp.broadcast_to(jnp.asarray(SLOPES_B, F32)[:, None], (N_HEADS_B, hd))
    lam_spec = pl.BlockSpec((None, 1, DQ_B), lambda hh, i: (layer, 0, 0))
    assert s_s % (2 * tk) == 0, "key tiles are processed in pairs"
    tl = min(tq, BIAS_LANES)
    assert tk % tl == 0 and tq % tl == 0 and s_p % tk == 0
    bias_rows = 2 * s_p
    est = 4 * s_s * hd * 2 + 2 * s_p * hd * 2 + bias_rows * tl * 4 + 8 * 2 * tq * tk * 4
    return pl.pallas_call(
        functools.partial(_diff_body, tq=tq, tk=tk, seqs=seqs, lam_init=lam_init),
        grid=(N_HEADS_B, t // tq),
        in_specs=[
            pl.BlockSpec((tq, hd), lambda hh, i: (i, gq + hh)),
            *_kv_specs(tq, seqs, gk, gv),
            pl.BlockSpec((N_HEADS_B, hd), lambda hh, i: (0, 0)),
            lam_spec, lam_spec, lam_spec, lam_spec,
            pl.BlockSpec((None, hd, 1), lambda hh, i: (layer, 0, 0)),
        ],
        out_specs=pl.BlockSpec((tq, hd), lambda hh, i: (i, hh)),
        out_shape=jax.ShapeDtypeStruct((t, N_HEADS_B * hd), BF16),
        scratch_shapes=[pltpu.VMEM((2 * tq, hd), BF16), pltpu.VMEM((s_p, hd), BF16),
                        pltpu.VMEM((hd + ONES_ROWS, s_p), BF16), pltpu.VMEM((bias_rows, tl), F32),
                        pltpu.VMEM((2, tk, 2 * tq), F32), pltpu.VMEM((hd + ONES_ROWS, 2 * tq), F32)],
        compiler_params=pltpu.CompilerParams(
            dimension_semantics=("arbitrary", "arbitrary"), vmem_limit_bytes=_vmem_limit(est)),
        name="diff_attn",
    )(h, h, h, h, h, slopes, lq1, lk1, lq2, lk2, gt)


V7X_VMEM_BYTES = 64 * MIB
V7X_VMEM_CEILING = V7X_VMEM_BYTES - 6 * MIB


def _vmem_limit(estimate_bytes):
    return int(min(max(2 * estimate_bytes, 32 * MIB), V7X_VMEM_CEILING))


def _largest_tile(n, cap, step):
    best = None
    for c in range(step, min(n, cap) + 1, step):
        if n % c == 0:
            best = c
    assert best is not None, (n, cap, step)
    return best


def _plan(t, d_ff, s_s):
    return dict(
        tm=_largest_tile(t, 512, 128),
        tm_ffn=_largest_tile(t, 768, 128),
        tm_proj=_largest_tile(t, 1024, 128),
        tf=_largest_tile(d_ff, 512, 128),
        tq_dil=_largest_tile(s_s, 512, 128),
        tk_dil=_largest_tile(s_s, 512, 128),
        tq_diff=_largest_tile(s_s, 512, 128),
        tk_diff=_largest_tile(s_s, 512, 128),
    )


def kernel(x_prompt, x_sample, ln1_g, ln1_b, ffn1_gate, ffn1_up, ffn1_down, w_in, norm_a_g, lam_q1, lam_k1,
           lam_q2, lam_k2, subln_g, w_out, ln2_g, ln2_b, ffn2_gate, ffn2_up, ffn2_down, ln3_g, ln3_b):
    n_p, s_p, d = x_prompt.shape
    n_s, s_s, _ = x_sample.shape
    depth = w_in.shape[0]
    assert w_in.shape[-1] == N_GROUPS * GROUP_WIDTH
    seqs = (n_p, s_p, s_s)
    p_tot = n_p * s_p
    x = jnp.concatenate([x_prompt.reshape(p_tot, d), x_sample.reshape(n_s * s_s, d)], axis=0)
    t = x.shape[0]
    plan = _plan(t, ffn1_gate.shape[-1], s_s)
    alpha = (2 * depth) ** 0.25

    row = lambda v: v.reshape(depth, 1, v.shape[-1])
    col = lambda v: v.reshape(depth, v.shape[-1], 1)
    bf = lambda v: v.astype(BF16)
    ffn1 = (bf(ffn1_gate), bf(ffn1_up), bf(ffn1_down), row(ln1_g), row(ln1_b))
    ffn2 = (bf(ffn2_gate), bf(ffn2_up), bf(ffn2_down), row(ln3_g), row(ln3_b))
    w_in_b, w_out_b = bf(w_in), bf(w_out)
    ln2 = (row(ln2_g), row(ln2_b))
    gat, gbt = col(norm_a_g), col(subln_g)
    lams = [row(v) for v in (lam_q1, lam_k1, lam_q2, lam_k2)]

    for i in range(depth):
        x = _ffn(x, *ffn1, i, alpha, plan["tm_ffn"], plan["tf"])
        h = _in_proj(x, w_in_b, i, plan["tm_proj"])
        ya = _dilated_attention(h, gat, i, plan["tq_dil"], plan["tk_dil"], seqs)
        lam_init = 0.8 - 0.6 * math.exp(-0.3 * i)
        yb = _diff_attention(h, *lams, gbt, i, lam_init, plan["tq_diff"], plan["tk_diff"], seqs)
        x = _out_proj(x, ya, yb, w_out_b, *ln2, i, alpha, plan["tm"])
        x = _ffn(x, *ffn2, i, alpha, plan["tm_ffn"], plan["tf"])
    return x[:p_tot].reshape(x_prompt.shape), x[p_tot:].reshape(x_sample.shape)
```

```python
import functools
import math

import jax
import jax.numpy as jnp
from jax import lax
from jax.experimental import pallas as pl
from jax.experimental.pallas import tpu as pltpu

F32 = jnp.float32
BF16 = jnp.bfloat16

HEAD_DIM = 128
N_HEADS_A = 8
N_HEADS_B = 8
DQ_B = HEAD_DIM // 2
GROUP_WIDTH = N_HEADS_A * HEAD_DIM
N_GROUPS = 6
DILATED_CONFIGS = ((128, 1), (512, 4), (2048, 16))
ALIBI_MAX_BIAS = 8.0
LN_EPS = 1e-5
RMS_EPS = 1e-6
NEG_INF = -1e30

LANES = 128
MIB = 1024 * 1024
NT_DIMS = (((1,), (1,)), ((), ()))
LOG2E = math.log2(math.e)
Q_SCALE_A = HEAD_DIM ** -0.5 * LOG2E
Q_SCALE_B = DQ_B ** -0.5 * LOG2E


def _alibi_slope(n):
    return 2.0 ** (-ALIBI_MAX_BIAS * n / (N_HEADS_A + N_HEADS_B))


SLOPES_A = tuple(_alibi_slope(2 * h + 1) for h in range(N_HEADS_A))
SLOPES_B = tuple(_alibi_slope(2 * h + 2) for h in range(N_HEADS_B))


def _layer_norm(y, g, b):
    mu = jnp.mean(y, axis=-1, keepdims=True)
    d = y - mu
    var = jnp.mean(d * d, axis=-1, keepdims=True)
    return d * lax.rsqrt(var + LN_EPS) * g + b


def _ffn_body(x_ref, wg_ref, wu_ref, wd_ref, g_ref, b_ref, o_ref, xb_ref, acc_ref, *, alpha):
    j = pl.program_id(1)

    @pl.when(j == 0)
    def _():
        xb_ref[...] = x_ref[...].astype(BF16)
        acc_ref[...] = jnp.zeros_like(acc_ref)

    xb = xb_ref[...]
    gate = jnp.dot(xb, wg_ref[...], preferred_element_type=F32)
    up = jnp.dot(xb, wu_ref[...], preferred_element_type=F32)
    act = (gate * jax.nn.sigmoid(gate) * up).astype(BF16)
    acc_ref[...] += jnp.dot(act, wd_ref[...], preferred_element_type=F32)

    @pl.when(j == pl.num_programs(1) - 1)
    def _():
        y = alpha * x_ref[...] + 0.5 * acc_ref[...]
        o_ref[...] = _layer_norm(y, g_ref[...], b_ref[...])


def _ffn(x, wg, wu, wd, g, b, layer, alpha, tm, tf):
    t, d = x.shape
    f = wg.shape[-1]
    est = 2 * 2 * tm * d * 4 + 2 * 3 * d * tf * 2 + tm * d * 6 + 4 * tm * tf * 4
    return pl.pallas_call(
        functools.partial(_ffn_body, alpha=alpha),
        grid=(t // tm, f // tf),
        in_specs=[
            pl.BlockSpec((tm, d), lambda i, j: (i, 0)),
            pl.BlockSpec((None, d, tf), lambda i, j: (layer, 0, j)),
            pl.BlockSpec((None, d, tf), lambda i, j: (layer, 0, j)),
            pl.BlockSpec((None, tf, d), lambda i, j: (layer, j, 0)),
            pl.BlockSpec((None, 1, d), lambda i, j: (layer, 0, 0)),
            pl.BlockSpec((None, 1, d), lambda i, j: (layer, 0, 0)),
        ],
        out_specs=pl.BlockSpec((tm, d), lambda i, j: (i, 0)),
        out_shape=jax.ShapeDtypeStruct((t, d), F32),
        scratch_shapes=[pltpu.VMEM((tm, d), BF16), pltpu.VMEM((tm, d), F32)],
        compiler_params=pltpu.CompilerParams(
            dimension_semantics=("parallel", "arbitrary"), vmem_limit_bytes=_vmem_limit(est)),
        name="ffn",
    )(x, wg, wu, wd, g, b)


def _in_proj_body(x_ref, w_ref, o_ref, xb_ref):
    j = pl.program_id(1)

    @pl.when(j == 0)
    def _():
        xb_ref[...] = x_ref[...].astype(BF16)

    scale = jnp.where(j == 0, Q_SCALE_A, jnp.where(j == N_GROUPS // 2, Q_SCALE_B, 1.0)).astype(F32)
    o_ref[...] = (jnp.dot(xb_ref[...], w_ref[...], preferred_element_type=F32) * scale).astype(BF16)


def _in_proj(x, w, layer, tm):
    t, d = x.shape
    n = w.shape[-1]
    tn = GROUP_WIDTH
    est = 2 * tm * d * 4 + 2 * d * tn * 2 + 2 * tm * tn * 2 + tm * d * 2 + tm * tn * 4
    return pl.pallas_call(
        _in_proj_body,
        grid=(t // tm, n // tn),
        in_specs=[
            pl.BlockSpec((tm, d), lambda i, j: (i, 0)),
            pl.BlockSpec((None, d, tn), lambda i, j: (layer, 0, j)),
        ],
        out_specs=pl.BlockSpec((tm, tn), lambda i, j: (i, j)),
        out_shape=jax.ShapeDtypeStruct((t, n), BF16),
        scratch_shapes=[pltpu.VMEM((tm, d), BF16)],
        compiler_params=pltpu.CompilerParams(
            dimension_semantics=("parallel", "arbitrary"), vmem_limit_bytes=_vmem_limit(est)),
        name="in_proj",
    )(x, w)


def _out_proj_body(x_ref, ya_ref, yb_ref, w_ref, g_ref, b_ref, o_ref, *, alpha):
    wa = ya_ref.shape[-1]
    y = jnp.dot(ya_ref[...], w_ref[:wa, :], preferred_element_type=F32)
    y = y + jnp.dot(yb_ref[...], w_ref[wa:, :], preferred_element_type=F32)
    o_ref[...] = _layer_norm(alpha * x_ref[...] + y, g_ref[...], b_ref[...])


def _out_proj(x, ya, yb, w, g, b, layer, alpha, tm):
    t, d = x.shape
    wa, wb = ya.shape[-1], yb.shape[-1]
    est = 2 * 2 * tm * d * 4 + 2 * tm * (wa + wb) * 2 + 2 * (wa + wb) * d * 2 + 2 * tm * d * 4
    return pl.pallas_call(
        functools.partial(_out_proj_body, alpha=alpha),
        grid=(t // tm,),
        in_specs=[
            pl.BlockSpec((tm, d), lambda i: (i, 0)),
            pl.BlockSpec((tm, wa), lambda i: (i, 0)),
            pl.BlockSpec((tm, wb), lambda i: (i, 0)),
            pl.BlockSpec((None, wa + wb, d), lambda i: (layer, 0, 0)),
            pl.BlockSpec((None, 1, d), lambda i: (layer, 0, 0)),
            pl.BlockSpec((None, 1, d), lambda i: (layer, 0, 0)),
        ],
        out_specs=pl.BlockSpec((tm, d), lambda i: (i, 0)),
        out_shape=jax.ShapeDtypeStruct((t, d), F32),
        compiler_params=pltpu.CompilerParams(
            dimension_semantics=("parallel",), vmem_limit_bytes=_vmem_limit(est)),
        name="out_proj",
    )(x, ya, yb, w, g, b)


def _seq_bounds(t0, seqs):
    n_p, s_p, s_s = seqs
    p_tot = n_p * s_p
    in_prompt = t0 < p_tot
    lo_p = (t0 // s_p) * s_p
    lo_s = p_tot + ((jnp.maximum(t0, p_tot) - p_tot) // s_s) * s_s
    return jnp.where(in_prompt, lo_p, lo_s), jnp.where(in_prompt, s_p, s_s)


def _kv_block_maps(tq, seqs):
    n_p, s_p, s_s = seqs
    assert s_p in (s_s, 2 * s_s), "a sequence is held as at most two blocks of s_s rows"
    per_seq = s_p // s_s
    tiles_per_blk = s_s // tq
    p_blks = n_p * per_seq

    def lo_blk(i):
        b = i // tiles_per_blk
        return jnp.where(b < p_blks, (b // per_seq) * per_seq, b)

    def hi_blk(i):
        b = i // tiles_per_blk
        return jnp.where(b < p_blks, (b // per_seq) * per_seq + (per_seq - 1), b)

    return lo_blk, hi_blk


def _kv_specs(tq, seqs, k_col, v_col):
    s_s = seqs[2]
    lo_blk, hi_blk = _kv_block_maps(tq, seqs)

    def spec(blk, col):
        return pl.BlockSpec((s_s, HEAD_DIM), lambda hh, i: (blk(i), col + hh), pipeline_mode=pl.Buffered(1))

    return [spec(lo_blk, k_col), spec(hi_blk, k_col), spec(lo_blk, v_col), spec(hi_blk, v_col)]


ONES_ROWS = 16
BIAS_LANES = 256


def _load_sequence(length, klo_ref, khi_ref, vlo_ref, vhi_ref, kseq_ref, vt_ref, s_s, two_blocks):
    def fill(k_ref, v_ref, base):
        kseq_ref[base:base + s_s, :] = k_ref[...]
        vt_ref[HEAD_DIM:, base:base + s_s] = jnp.ones((ONES_ROWS, s_s), vt_ref.dtype)
        for c in range(s_s // LANES):
            rows = slice(c * LANES, (c + 1) * LANES)
            vt_ref[:HEAD_DIM, base + c * LANES:base + (c + 1) * LANES] = v_ref[rows, :].T

    fill(klo_ref, vlo_ref, 0)
    if two_blocks:
        @pl.when(length > s_s)
        def _():
            fill(khi_ref, vhi_ref, s_s)


def _finish_head(ot, gt_ref, gain, o_ref):
    ms = jnp.mean(ot * ot, axis=0, keepdims=True)
    yt = ot * lax.rsqrt(ms + RMS_EPS) * gt_ref[...]
    if gain != 1.0:
        yt = yt * gain
    o_ref[...] = yt.T.astype(o_ref.dtype)


def _dilated_windows(tq, tk):
    out = []
    for window, dil in DILATED_CONFIGS:
        reach = ((window // 2) // dil) * dil
        halo = -(-reach // (tk // 2)) * (tk // 2)
        out.append((dil, reach, halo, tq + 2 * halo))
    return out


def _dilated_bias_rows(halo, width, tq, tl):
    return width + 2 * halo + tq - tl


def _dilated_body(q_ref, klo_ref, khi_ref, vlo_ref, vhi_ref, slope_ref, gt_ref, o_ref, kseq_ref, vt_ref,
                  bias_ref, s_ref, acc_ref, *, tq, tk, seqs):
    n_p, s_p, s_s = seqs
    h = pl.program_id(0)
    i = pl.program_id(1)
    t0 = i * tq
    lo, length = _seq_bounds(t0, seqs)
    qpos0 = t0 - lo
    windows = _dilated_windows(tq, tk)

    tl = bias_ref.shape[1]
    bases, origins = [], []
    base = 0
    for dil, reach, halo, width in windows:
        bases.append(base)
        origins.append(2 * halo + tq - tl)
        base += _dilated_bias_rows(halo, width, tq, tl)

    @pl.when(i == 0)
    def _():
        slope = slope_ref[pl.ds(h, 1), :][:, :1] * LOG2E
        for (dil, reach, halo, width), b0, origin in zip(windows, bases, origins):
            shape = (_dilated_bias_rows(halo, width, tq, tl), tl)
            d = (lax.broadcasted_iota(jnp.int32, shape, 0) - lax.broadcasted_iota(jnp.int32, shape, 1)
                 - origin)
            dist = jnp.abs(d.astype(F32))
            valid = dist <= float(reach)
            if dil > 1:
                valid = valid & ((d & (dil - 1)) == 0)
            bias_ref[b0:b0 + shape[0], :] = jnp.where(valid, -slope * dist, NEG_INF)

    @pl.when(qpos0 == 0)
    def _():
        _load_sequence(length, klo_ref, khi_ref, vlo_ref, vhi_ref, kseq_ref, vt_ref, s_s, s_p > s_s)

    firsts, key0, bias0 = [], [], []
    n_tiles = 0
    for (dil, reach, halo, width), b0, origin in zip(windows, bases, origins):
        w0 = jnp.clip(qpos0 - halo, 0, length - width)
        firsts.append(n_tiles)
        key0.append(w0)
        bias0.append(b0 + w0 - qpos0 + origin)
        n_tiles += width // tk

    def offsets(j):
        k_off, b_off = key0[0] + j * tk, bias0[0] + j * tk
        for first, k0, b0 in zip(firsts[1:], key0[1:], bias0[1:]):
            k_off = jnp.where(j >= first, k0 + (j - first) * tk, k_off)
            b_off = jnp.where(j >= first, b0 + (j - first) * tk, b_off)
        return pl.multiple_of(k_off, tk // 2), pl.multiple_of(b_off, tk // 2)

    q = q_ref[...]
    acc_ref[...] = jnp.zeros_like(acc_ref)

    def score(j, slot):
        k_off, b_off = offsets(j)
        st = lax.dot_general(kseq_ref[pl.ds(k_off, tk), :], q, NT_DIMS, preferred_element_type=F32)
        bias = [bias_ref[pl.ds(pl.multiple_of(b_off - g * tl, tl), tk), :] for g in range(tq // tl)]
        st = st + jnp.concatenate(bias, axis=1)
        s_ref[slot] = st
        return jnp.max(st, axis=0, keepdims=True)

    def accumulate(j, slot, m_old, m_tile):
        k_off, _ = offsets(j)
        m_new = jnp.maximum(m_old, m_tile)
        p = jnp.exp2(s_ref[slot] - m_new).astype(BF16)
        acc_ref[...] = (acc_ref[...] * jnp.exp2(m_old - m_new)
                        + jnp.dot(vt_ref[:, pl.ds(k_off, tk)], p, preferred_element_type=F32))
        return m_new

    def pair(jj, carry):
        m_run, m_even = carry
        j = 2 * jj
        m_odd = score(j + 1, 1)
        m_run = accumulate(j, 0, m_run, m_even)
        m_even = score(j + 2, 0)
        m_run = accumulate(j + 1, 1, m_run, m_odd)
        return m_run, m_even

    m_run = jnp.full((1, tq), NEG_INF, F32)
    if n_tiles % 2:
        m_run, m_last = lax.fori_loop(0, n_tiles // 2, pair, (m_run, score(0, 0)))
        accumulate(n_tiles - 1, 0, m_run, m_last)
    else:
        m_run, m_even = lax.fori_loop(0, n_tiles // 2 - 1, pair, (m_run, score(0, 0)))
        m_odd = score(n_tiles - 1, 1)
        m_run = accumulate(n_tiles - 2, 0, m_run, m_even)
        accumulate(n_tiles - 1, 1, m_run, m_odd)
    _finish_head(acc_ref[:HEAD_DIM, :] / acc_ref[HEAD_DIM:HEAD_DIM + 1, :], gt_ref, 1.0, o_ref)


def _dilated_attention(h, gt, layer, tq, tk, seqs):
    t = h.shape[0]
    n_p, s_p, s_s = seqs
    hd = HEAD_DIM
    assert tq % tk == 0
    for dil, _, _, width in _dilated_windows(tq, tk):
        assert dil & (dil - 1) == 0 and width <= s_s and width % tk == 0
    gq, gk, gv = 0, N_HEADS_A, 2 * N_HEADS_A
    slopes = jnp.broadcast_to(jnp.asarray(SLOPES_A, F32)[:, None], (N_HEADS_A, hd))
    tl = min(tq, BIAS_LANES)
    assert tq % tl == 0 and (tk // 2) % tl == 0
    bias_rows = sum(_dilated_bias_rows(halo, width, tq, tl) for _, _, halo, width in _dilated_windows(tq, tk))
    est = 4 * s_s * hd * 2 + 2 * s_p * hd * 2 + bias_rows * tl * 4 + 8 * tk * tq * 4
    return pl.pallas_call(
        functools.partial(_dilated_body, tq=tq, tk=tk, seqs=seqs),
        grid=(N_HEADS_A, t // tq),
        in_specs=[
            pl.BlockSpec((tq, hd), lambda hh, i: (i, gq + hh)),
            *_kv_specs(tq, seqs, gk, gv),
            pl.BlockSpec((N_HEADS_A, hd), lambda hh, i: (0, 0)),
            pl.BlockSpec((None, hd, 1), lambda hh, i: (layer, 0, 0)),
        ],
        out_specs=pl.BlockSpec((tq, hd), lambda hh, i: (i, hh)),
        out_shape=jax.ShapeDtypeStruct((t, N_HEADS_A * hd), BF16),
        scratch_shapes=[pltpu.VMEM((s_p, hd), BF16), pltpu.VMEM((hd + ONES_ROWS, s_p), BF16),
                        pltpu.VMEM((bias_rows, tl), F32), pltpu.VMEM((2, tk, tq), F32),
                        pltpu.VMEM((hd + ONES_ROWS, tq), F32)],
        compiler_params=pltpu.CompilerParams(
            dimension_semantics=("arbitrary", "arbitrary"), vmem_limit_bytes=_vmem_limit(est)),
        name="dilated_attn",
    )(h, h, h, h, h, slopes, gt)


def _diff_body(q_ref, klo_ref, khi_ref, vlo_ref, vhi_ref, slope_ref, lq1_ref, lk1_ref, lq2_ref, lk2_ref,
               gt_ref, o_ref, qbd_ref, kseq_ref, vt_ref, bias_ref, s_ref, acc_ref,
               *, tq, tk, seqs, lam_init):
    n_p, s_p, s_s = seqs
    h = pl.program_id(0)
    i = pl.program_id(1)
    t0 = i * tq
    lo, length = _seq_bounds(t0, seqs)
    qpos0 = t0 - lo
    tl = bias_ref.shape[1]
    origin = s_p - tl

    @pl.when(i == 0)
    def _():
        slope = slope_ref[pl.ds(h, 1), :][:, :1] * LOG2E
        rows = lax.broadcasted_iota(jnp.int32, (tk, tl), 0) - lax.broadcasted_iota(jnp.int32, (tk, tl), 1)

        def fill(c, carry):
            r0 = pl.multiple_of(c * tk, tk)
            bias_ref[pl.ds(r0, tk), :] = -slope * jnp.abs((rows + (r0 - origin)).astype(F32))
            return carry

        lax.fori_loop(0, bias_ref.shape[0] // tk, fill, 0)

    @pl.when(qpos0 == 0)
    def _():
        _load_sequence(length, klo_ref, khi_ref, vlo_ref, vhi_ref, kseq_ref, vt_ref, s_s, s_p > s_s)

    q = q_ref[...]
    lane = lax.broadcasted_iota(jnp.int32, q.shape, 1)
    zero = jnp.zeros_like(q)
    qbd_ref[0:tq, :] = jnp.where(lane < DQ_B, q, zero)
    qbd_ref[tq:2 * tq, :] = jnp.where(lane >= DQ_B, q, zero)
    acc_ref[...] = jnp.zeros_like(acc_ref)

    def score(j, slot):
        start = pl.multiple_of(j * tk, tk)
        st = lax.dot_general(kseq_ref[pl.ds(start, tk), :], qbd_ref[...], NT_DIMS,
                             preferred_element_type=F32)
        row0 = start - qpos0 + origin
        bias = [bias_ref[pl.ds(pl.multiple_of(row0 - g * tl, tl), tk), :] for g in range(tq // tl)]
        st = st + jnp.concatenate(bias + bias, axis=1)
        s_ref[slot] = st
        return jnp.max(st, axis=0, keepdims=True)

    def accumulate(j, slot, m_old, m_tile):
        m_new = jnp.maximum(m_old, m_tile)
        p = jnp.exp2(s_ref[slot] - m_new).astype(BF16)
        vt = vt_ref[:, pl.ds(pl.multiple_of(j * tk, tk), tk)]
        acc_ref[...] = (acc_ref[...] * jnp.exp2(m_old - m_new)
                        + jnp.dot(vt, p, preferred_element_type=F32))
        return m_new

    def pair(jj, carry):
        m_run, m_even = carry
        j = 2 * jj
        m_odd = score(j + 1, 1)
        m_run = accumulate(j, 0, m_run, m_even)
        m_even = score(j + 2, 0)
        m_run = accumulate(j + 1, 1, m_run, m_odd)
        return m_run, m_even

    n_pairs = length // (2 * tk)
    m_run = jnp.full((1, 2 * tq), NEG_INF, F32)
    m_run, m_even = lax.fori_loop(0, n_pairs - 1, pair, (m_run, score(0, 0)))
    j_last = 2 * (n_pairs - 1)
    m_odd = score(j_last + 1, 1)
    m_run = accumulate(j_last, 0, m_run, m_even)
    accumulate(j_last + 1, 1, m_run, m_odd)

    lam = (jnp.exp(jnp.sum(lq1_ref[...] * lk1_ref[...], axis=-1, keepdims=True))
           - jnp.exp(jnp.sum(lq2_ref[...] * lk2_ref[...], axis=-1, keepdims=True)) + lam_init)
    ot = acc_ref[:HEAD_DIM, :] / acc_ref[HEAD_DIM:HEAD_DIM + 1, :]
    _finish_head(ot[:, :tq] - lam * ot[:, tq:], gt_ref, 1.0 - lam_init, o_ref)


def _diff_attention(h, lq1, lk1, lq2, lk2, gt, layer, lam_init, tq, tk, seqs):
    t = h.shape[0]
    n_p, s_p, s_s = seqs
    hd = HEAD_DIM
    gq, gk, gv = 3 * N_HEADS_B, 4 * N_HEADS_B, 5 * N_HEADS_B
    slopes = jnp.broadcast_to(jnp.asarray(SLOPES_B, F32)[:, None], (N_HEADS_B, hd))
    lam_spec = pl.BlockSpec((None, 1, DQ_B), lambda hh, i: (layer, 0, 0))
    assert s_s % (2 * tk) == 0, "key tiles are processed in pairs"
    tl = min(tq, BIAS_LANES)
    assert tk % tl == 0 and tq % tl == 0 and s_p % tk == 0
    bias_rows = 2 * s_p
    est = 4 * s_s * hd * 2 + 2 * s_p * hd * 2 + bias_rows * tl * 4 + 8 * 2 * tq * tk * 4
    return pl.pallas_call(
        functools.partial(_diff_body, tq=tq, tk=tk, seqs=seqs, lam_init=lam_init),
        grid=(N_HEADS_B, t // tq),
        in_specs=[
            pl.BlockSpec((tq, hd), lambda hh, i: (i, gq + hh)),
            *_kv_specs(tq, seqs, gk, gv),
            pl.BlockSpec((N_HEADS_B, hd), lambda hh, i: (0, 0)),
            lam_spec, lam_spec, lam_spec, lam_spec,
            pl.BlockSpec((None, hd, 1), lambda hh, i: (layer, 0, 0)),
        ],
        out_specs=pl.BlockSpec((tq, hd), lambda hh, i: (i, hh)),
        out_shape=jax.ShapeDtypeStruct((t, N_HEADS_B * hd), BF16),
        scratch_shapes=[pltpu.VMEM((2 * tq, hd), BF16), pltpu.VMEM((s_p, hd), BF16),
                        pltpu.VMEM((hd + ONES_ROWS, s_p), BF16), pltpu.VMEM((bias_rows, tl), F32),
                        pltpu.VMEM((2, tk, 2 * tq), F32), pltpu.VMEM((hd + ONES_ROWS, 2 * tq), F32)],
        compiler_params=pltpu.CompilerParams(
            dimension_semantics=("arbitrary", "arbitrary"), vmem_limit_bytes=_vmem_limit(est)),
        name="diff_attn",
    )(h, h, h, h, h, slopes, lq1, lk1, lq2, lk2, gt)


V7X_VMEM_BYTES = 64 * MIB
V7X_VMEM_CEILING = V7X_VMEM_BYTES - 6 * MIB


def _vmem_limit(estimate_bytes):
    return int(min(max(2 * estimate_bytes, 32 * MIB), V7X_VMEM_CEILING))


def _largest_tile(n, cap, step):
    best = None
    for c in range(step, min(n, cap) + 1, step):
        if n % c == 0:
            best = c
    assert best is not None, (n, cap, step)
    return best


def _plan(t, d_ff, s_s):
    return dict(
        tm=_largest_tile(t, 512, 128),
        tm_ffn=_largest_tile(t, 768, 128),
        tm_proj=_largest_tile(t, 1024, 128),
        tf=_largest_tile(d_ff, 512, 128),
        tq_dil=_largest_tile(s_s, 512, 128),
        tk_dil=_largest_tile(s_s, 512, 128),
        tq_diff=_largest_tile(s_s, 2048, 128),
        tk_diff=_largest_tile(s_s, 512, 128),
    )


def kernel(x_prompt, x_sample, ln1_g, ln1_b, ffn1_gate, ffn1_up, ffn1_down, w_in, norm_a_g, lam_q1, lam_k1,
           lam_q2, lam_k2, subln_g, w_out, ln2_g, ln2_b, ffn2_gate, ffn2_up, ffn2_down, ln3_g, ln3_b):
    n_p, s_p, d = x_prompt.shape
    n_s, s_s, _ = x_sample.shape
    depth = w_in.shape[0]
    assert w_in.shape[-1] == N_GROUPS * GROUP_WIDTH
    seqs = (n_p, s_p, s_s)
    p_tot = n_p * s_p
    x = jnp.concatenate([x_prompt.reshape(p_tot, d), x_sample.reshape(n_s * s_s, d)], axis=0)
    t = x.shape[0]
    plan = _plan(t, ffn1_gate.shape[-1], s_s)
    alpha = (2 * depth) ** 0.25

    row = lambda v: v.reshape(depth, 1, v.shape[-1])
    col = lambda v: v.reshape(depth, v.shape[-1], 1)
    bf = lambda v: v.astype(BF16)
    ffn1 = (bf(ffn1_gate), bf(ffn1_up), bf(ffn1_down), row(ln1_g), row(ln1_b))
    ffn2 = (bf(ffn2_gate), bf(ffn2_up), bf(ffn2_down), row(ln3_g), row(ln3_b))
    w_in_b, w_out_b = bf(w_in), bf(w_out)
    ln2 = (row(ln2_g), row(ln2_b))
    gat, gbt = col(norm_a_g), col(subln_g)
    lams = [row(v) for v in (lam_q1, lam_k1, lam_q2, lam_k2)]

    for i in range(depth):
        x = _ffn(x, *ffn1, i, alpha, plan["tm_ffn"], plan["tf"])
        h = _in_proj(x, w_in_b, i, plan["tm_proj"])
        ya = _dilated_attention(h, gat, i, plan["tq_dil"], plan["tk_dil"], seqs)
        lam_init = 0.8 - 0.6 * math.exp(-0.3 * i)
        yb = _diff_attention(h, *lams, gbt, i, lam_init, plan["tq_diff"], plan["tk_diff"], seqs)
        x = _out_proj(x, ya, yb, w_out_b, *ln2, i, alpha, plan["tm"])
        x = _ffn(x, *ffn2, i, alpha, plan["tm_ffn"], plan["tf"])
    return x[:p_tot].reshape(x_prompt.shape), x[p_tot:].reshape(x_sample.shape)
```

```python
import functools
import math

import jax
import jax.numpy as jnp
from jax import lax
from jax.experimental import pallas as pl
from jax.experimental.pallas import tpu as pltpu

F32 = jnp.float32
BF16 = jnp.bfloat16

HEAD_DIM = 128
N_HEADS_A = 8
N_HEADS_B = 8
DQ_B = HEAD_DIM // 2
GROUP_WIDTH = N_HEADS_A * HEAD_DIM
N_GROUPS = 6
DILATED_CONFIGS = ((128, 1), (512, 4), (2048, 16))
ALIBI_MAX_BIAS = 8.0
LN_EPS = 1e-5
RMS_EPS = 1e-6
NEG_INF = -1e30

LANES = 128
MIB = 1024 * 1024
NT_DIMS = (((1,), (1,)), ((), ()))
LOG2E = math.log2(math.e)
Q_SCALE_A = HEAD_DIM ** -0.5 * LOG2E
Q_SCALE_B = DQ_B ** -0.5 * LOG2E


def _alibi_slope(n):
    return 2.0 ** (-ALIBI_MAX_BIAS * n / (N_HEADS_A + N_HEADS_B))


SLOPES_A = tuple(_alibi_slope(2 * h + 1) for h in range(N_HEADS_A))
SLOPES_B = tuple(_alibi_slope(2 * h + 2) for h in range(N_HEADS_B))


def _layer_norm(y, g, b):
    mu = jnp.mean(y, axis=-1, keepdims=True)
    d = y - mu
    var = jnp.mean(d * d, axis=-1, keepdims=True)
    return d * lax.rsqrt(var + LN_EPS) * g + b


def _ffn_body(x_ref, wg_ref, wu_ref, wd_ref, g_ref, b_ref, o_ref, xb_ref, acc_ref, *, alpha):
    j = pl.program_id(1)

    @pl.when(j == 0)
    def _():
        xb_ref[...] = x_ref[...].astype(BF16)
        acc_ref[...] = jnp.zeros_like(acc_ref)

    xb = xb_ref[...]
    gate = jnp.dot(xb, wg_ref[...], preferred_element_type=F32)
    up = jnp.dot(xb, wu_ref[...], preferred_element_type=F32)
    act = (gate * jax.nn.sigmoid(gate) * up).astype(BF16)
    acc_ref[...] += jnp.dot(act, wd_ref[...], preferred_element_type=F32)

    @pl.when(j == pl.num_programs(1) - 1)
    def _():
        y = alpha * x_ref[...] + 0.5 * acc_ref[...]
        o_ref[...] = _layer_norm(y, g_ref[...], b_ref[...])


def _ffn(x, wg, wu, wd, g, b, layer, alpha, tm, tf):
    t, d = x.shape
    f = wg.shape[-1]
    est = 2 * 2 * tm * d * 4 + 2 * 3 * d * tf * 2 + tm * d * 6 + 4 * tm * tf * 4
    return pl.pallas_call(
        functools.partial(_ffn_body, alpha=alpha),
        grid=(t // tm, f // tf),
        in_specs=[
            pl.BlockSpec((tm, d), lambda i, j: (i, 0)),
            pl.BlockSpec((None, d, tf), lambda i, j: (layer, 0, j)),
            pl.BlockSpec((None, d, tf), lambda i, j: (layer, 0, j)),
            pl.BlockSpec((None, tf, d), lambda i, j: (layer, j, 0)),
            pl.BlockSpec((None, 1, d), lambda i, j: (layer, 0, 0)),
            pl.BlockSpec((None, 1, d), lambda i, j: (layer, 0, 0)),
        ],
        out_specs=pl.BlockSpec((tm, d), lambda i, j: (i, 0)),
        out_shape=jax.ShapeDtypeStruct((t, d), F32),
        scratch_shapes=[pltpu.VMEM((tm, d), BF16), pltpu.VMEM((tm, d), F32)],
        compiler_params=pltpu.CompilerParams(
            dimension_semantics=("parallel", "arbitrary"), vmem_limit_bytes=_vmem_limit(est)),
        name="ffn",
    )(x, wg, wu, wd, g, b)


def _in_proj_body(x_ref, w_ref, o_ref, xb_ref):
    j = pl.program_id(1)

    @pl.when(j == 0)
    def _():
        xb_ref[...] = x_ref[...].astype(BF16)

    scale = jnp.where(j == 0, Q_SCALE_A, jnp.where(j == N_GROUPS // 2, Q_SCALE_B, 1.0)).astype(F32)
    o_ref[...] = (jnp.dot(xb_ref[...], w_ref[...], preferred_element_type=F32) * scale).astype(BF16)


def _in_proj(x, w, layer, tm):
    t, d = x.shape
    n = w.shape[-1]
    tn = GROUP_WIDTH
    est = 2 * tm * d * 4 + 2 * d * tn * 2 + 2 * tm * tn * 2 + tm * d * 2 + tm * tn * 4
    return pl.pallas_call(
        _in_proj_body,
        grid=(t // tm, n // tn),
        in_specs=[
            pl.BlockSpec((tm, d), lambda i, j: (i, 0)),
            pl.BlockSpec((None, d, tn), lambda i, j: (layer, 0, j)),
        ],
        out_specs=pl.BlockSpec((tm, tn), lambda i, j: (i, j)),
        out_shape=jax.ShapeDtypeStruct((t, n), BF16),
        scratch_shapes=[pltpu.VMEM((tm, d), BF16)],
        compiler_params=pltpu.CompilerParams(
            dimension_semantics=("parallel", "arbitrary"), vmem_limit_bytes=_vmem_limit(est)),
        name="in_proj",
    )(x, w)


def _out_proj_body(x_ref, ya_ref, yb_ref, w_ref, g_ref, b_ref, o_ref, *, alpha):
    wa = ya_ref.shape[-1]
    y = jnp.dot(ya_ref[...], w_ref[:wa, :], preferred_element_type=F32)
    y = y + jnp.dot(yb_ref[...], w_ref[wa:, :], preferred_element_type=F32)
    o_ref[...] = _layer_norm(alpha * x_ref[...] + y, g_ref[...], b_ref[...])


def _out_proj(x, ya, yb, w, g, b, layer, alpha, tm):
    t, d = x.shape
    wa, wb = ya.shape[-1], yb.shape[-1]
    est = 2 * 2 * tm * d * 4 + 2 * tm * (wa + wb) * 2 + 2 * (wa + wb) * d * 2 + 2 * tm * d * 4
    return pl.pallas_call(
        functools.partial(_out_proj_body, alpha=alpha),
        grid=(t // tm,),
        in_specs=[
            pl.BlockSpec((tm, d), lambda i: (i, 0)),
            pl.BlockSpec((tm, wa), lambda i: (i, 0)),
            pl.BlockSpec((tm, wb), lambda i: (i, 0)),
            pl.BlockSpec((None, wa + wb, d), lambda i: (layer, 0, 0)),
            pl.BlockSpec((None, 1, d), lambda i: (layer, 0, 0)),
            pl.BlockSpec((None, 1, d), lambda i: (layer, 0, 0)),
        ],
        out_specs=pl.BlockSpec((tm, d), lambda i: (i, 0)),
        out_shape=jax.ShapeDtypeStruct((t, d), F32),
        compiler_params=pltpu.CompilerParams(
            dimension_semantics=("parallel",), vmem_limit_bytes=_vmem_limit(est)),
        name="out_proj",
    )(x, ya, yb, w, g, b)


def _seq_bounds(t0, seqs):
    n_p, s_p, s_s = seqs
    p_tot = n_p * s_p
    in_prompt = t0 < p_tot
    lo_p = (t0 // s_p) * s_p
    lo_s = p_tot + ((jnp.maximum(t0, p_tot) - p_tot) // s_s) * s_s
    return jnp.where(in_prompt, lo_p, lo_s), jnp.where(in_prompt, s_p, s_s)


def _kv_block_maps(tq, seqs):
    n_p, s_p, s_s = seqs
    assert s_p in (s_s, 2 * s_s), "a sequence is held as at most two blocks of s_s rows"
    per_seq = s_p // s_s
    tiles_per_blk = s_s // tq
    p_blks = n_p * per_seq

    def lo_blk(i):
        b = i // tiles_per_blk
        return jnp.where(b < p_blks, (b // per_seq) * per_seq, b)

    def hi_blk(i):
        b = i // tiles_per_blk
        return jnp.where(b < p_blks, (b // per_seq) * per_seq + (per_seq - 1), b)

    return lo_blk, hi_blk


def _kv_specs(tq, seqs, k_col, v_col):
    s_s = seqs[2]
    lo_blk, hi_blk = _kv_block_maps(tq, seqs)

    def spec(blk, col):
        return pl.BlockSpec((s_s, HEAD_DIM), lambda hh, i: (blk(i), col + hh), pipeline_mode=pl.Buffered(1))

    return [spec(lo_blk, k_col), spec(hi_blk, k_col), spec(lo_blk, v_col), spec(hi_blk, v_col)]


ONES_ROWS = 16
BIAS_LANES = 256


def _load_sequence(length, klo_ref, khi_ref, vlo_ref, vhi_ref, kseq_ref, vt_ref, s_s, two_blocks):
    def fill(k_ref, v_ref, base):
        kseq_ref[base:base + s_s, :] = k_ref[...]
        vt_ref[HEAD_DIM:, base:base + s_s] = jnp.ones((ONES_ROWS, s_s), vt_ref.dtype)
        for c in range(s_s // LANES):
            rows = slice(c * LANES, (c + 1) * LANES)
            vt_ref[:HEAD_DIM, base + c * LANES:base + (c + 1) * LANES] = v_ref[rows, :].T

    fill(klo_ref, vlo_ref, 0)
    if two_blocks:
        @pl.when(length > s_s)
        def _():
            fill(khi_ref, vhi_ref, s_s)


def _finish_head(ot, gt_ref, gain, o_ref):
    ms = jnp.mean(ot * ot, axis=0, keepdims=True)
    yt = ot * lax.rsqrt(ms + RMS_EPS) * gt_ref[...]
    if gain != 1.0:
        yt = yt * gain
    o_ref[...] = yt.T.astype(o_ref.dtype)


def _dilated_windows(tq, tk):
    out = []
    for window, dil in DILATED_CONFIGS:
        reach = ((window // 2) // dil) * dil
        halo = -(-reach // BIAS_LANES) * BIAS_LANES
        while (tq + 2 * halo) % tk:
            halo += BIAS_LANES
        out.append((dil, reach, halo, tq + 2 * halo))
    return out


def _dilated_bias_rows(halo, width, tq, tl):
    return width + 2 * halo + tq - tl


def _dilated_body(q_ref, klo_ref, khi_ref, vlo_ref, vhi_ref, slope_ref, gt_ref, o_ref, kseq_ref, vt_ref,
                  bias_ref, s_ref, acc_ref, *, tq, tk, seqs):
    n_p, s_p, s_s = seqs
    n_sub = acc_ref.shape[0]
    h = pl.program_id(0)
    i = pl.program_id(1)
    t0 = i * (n_sub * tq)
    lo, length = _seq_bounds(t0, seqs)
    qpos0 = t0 - lo
    windows = _dilated_windows(tq, tk)

    tl = bias_ref.shape[1]
    bases, origins = [], []
    base = 0
    for dil, reach, halo, width in windows:
        bases.append(base)
        origins.append(2 * halo + tq - tl)
        base += _dilated_bias_rows(halo, width, tq, tl)

    @pl.when(i == 0)
    def _():
        slope = slope_ref[pl.ds(h, 1), :][:, :1] * LOG2E
        for (dil, reach, halo, width), b0, origin in zip(windows, bases, origins):
            shape = (_dilated_bias_rows(halo, width, tq, tl), tl)
            d = (lax.broadcasted_iota(jnp.int32, shape, 0) - lax.broadcasted_iota(jnp.int32, shape, 1)
                 - origin)
            dist = jnp.abs(d.astype(F32))
            valid = dist <= float(reach)
            if dil > 1:
                valid = valid & ((d & (dil - 1)) == 0)
            bias_ref[b0:b0 + shape[0], :] = jnp.where(valid, -slope * dist, NEG_INF)

    @pl.when(qpos0 == 0)
    def _():
        _load_sequence(length, klo_ref, khi_ref, vlo_ref, vhi_ref, kseq_ref, vt_ref, s_s, s_p > s_s)

    firsts = []
    per_query_tile = 0
    for dil, reach, halo, width in windows:
        firsts.append(per_query_tile)
        per_query_tile += width // tk
    n_tiles = n_sub * per_query_tile

    def offsets(j):
        sub = lax.div(j, per_query_tile)
        local = j - sub * per_query_tile
        qpos = qpos0 + sub * tq
        k_off = b_off = None
        for (dil, reach, halo, width), first, b0, origin in zip(windows, firsts, bases, origins):
            w0 = jnp.clip(qpos - halo, 0, length - width)
            k_br = w0 + (local - first) * tk
            b_br = b0 + w0 - qpos + origin + (local - first) * tk
            k_off = k_br if k_off is None else jnp.where(local >= first, k_br, k_off)
            b_off = b_br if b_off is None else jnp.where(local >= first, b_br, b_off)
        return sub, local, pl.multiple_of(k_off, tl), pl.multiple_of(b_off, tl)

    acc_ref[...] = jnp.zeros_like(acc_ref)

    def score(j, slot):
        sub, _, k_off, b_off = offsets(j)
        q = q_ref[pl.ds(pl.multiple_of(sub * tq, tq), tq), :]
        st = lax.dot_general(kseq_ref[pl.ds(k_off, tk), :], q, NT_DIMS, preferred_element_type=F32)
        bias = [bias_ref[pl.ds(pl.multiple_of(b_off - g * tl, tl), tk), :] for g in range(tq // tl)]
        st = st + jnp.concatenate(bias, axis=1)
        s_ref[slot] = st
        return jnp.max(st, axis=0, keepdims=True)

    def accumulate(j, slot, m_run, m_tile):
        sub, local, k_off, _ = offsets(j)
        m_old = jnp.where(local == 0, NEG_INF, m_run)
        m_new = jnp.maximum(m_old, m_tile)
        p = jnp.exp2(s_ref[slot] - m_new).astype(BF16)
        acc_ref[sub] = (acc_ref[sub] * jnp.exp2(m_old - m_new)
                        + jnp.dot(vt_ref[:, pl.ds(k_off, tk)], p, preferred_element_type=F32))
        return m_new

    def pair(jj, carry):
        m_run, m_even = carry
        j = 2 * jj
        m_odd = score(j + 1, 1)
        m_run = accumulate(j, 0, m_run, m_even)
        m_even = score(j + 2, 0)
        m_run = accumulate(j + 1, 1, m_run, m_odd)
        return m_run, m_even

    m_run = jnp.full((1, tq), NEG_INF, F32)
    if n_tiles % 2:
        m_run, m_last = lax.fori_loop(0, n_tiles // 2, pair, (m_run, score(0, 0)))
        accumulate(n_tiles - 1, 0, m_run, m_last)
    else:
        m_run, m_even = lax.fori_loop(0, n_tiles // 2 - 1, pair, (m_run, score(0, 0)))
        m_odd = score(n_tiles - 1, 1)
        m_run = accumulate(n_tiles - 2, 0, m_run, m_even)
        accumulate(n_tiles - 1, 1, m_run, m_odd)
    for sub in range(n_sub):
        _finish_head(acc_ref[sub, :HEAD_DIM, :] / acc_ref[sub, HEAD_DIM:HEAD_DIM + 1, :], gt_ref, 1.0,
                     o_ref.at[sub * tq:(sub + 1) * tq, :])


def _dilated_attention(h, gt, layer, tq, tk, n_sub, seqs):
    t = h.shape[0]
    n_p, s_p, s_s = seqs
    hd = HEAD_DIM
    tq_step = n_sub * tq
    assert s_s % tq_step == 0
    for dil, _, _, width in _dilated_windows(tq, tk):
        assert dil & (dil - 1) == 0 and width <= s_s and width % tk == 0
    gq, gk, gv = 0, N_HEADS_A, 2 * N_HEADS_A
    slopes = jnp.broadcast_to(jnp.asarray(SLOPES_A, F32)[:, None], (N_HEADS_A, hd))
    tl = min(tq, BIAS_LANES)
    assert tq % tl == 0 and tk % tl == 0 and tl == BIAS_LANES
    bias_rows = sum(_dilated_bias_rows(halo, width, tq, tl) for _, _, halo, width in _dilated_windows(tq, tk))
    est = 4 * s_s * hd * 2 + 2 * s_p * hd * 2 + bias_rows * tl * 4 + 8 * tk * tq * 4
    return pl.pallas_call(
        functools.partial(_dilated_body, tq=tq, tk=tk, seqs=seqs),
        grid=(N_HEADS_A, t // tq_step),
        in_specs=[
            pl.BlockSpec((tq_step, hd), lambda hh, i: (i, gq + hh)),
            *_kv_specs(tq_step, seqs, gk, gv),
            pl.BlockSpec((N_HEADS_A, hd), lambda hh, i: (0, 0)),
            pl.BlockSpec((None, hd, 1), lambda hh, i: (layer, 0, 0)),
        ],
        out_specs=pl.BlockSpec((tq_step, hd), lambda hh, i: (i, hh)),
        out_shape=jax.ShapeDtypeStruct((t, N_HEADS_A * hd), BF16),
        scratch_shapes=[pltpu.VMEM((s_p, hd), BF16), pltpu.VMEM((hd + ONES_ROWS, s_p), BF16),
                        pltpu.VMEM((bias_rows, tl), F32), pltpu.VMEM((2, tk, tq), F32),
                        pltpu.VMEM((n_sub, hd + ONES_ROWS, tq), F32)],
        compiler_params=pltpu.CompilerParams(
            dimension_semantics=("arbitrary", "arbitrary"), vmem_limit_bytes=_vmem_limit(est)),
        name="dilated_attn",
    )(h, h, h, h, h, slopes, gt)


def _diff_body(q_ref, klo_ref, khi_ref, vlo_ref, vhi_ref, slope_ref, lq1_ref, lk1_ref, lq2_ref, lk2_ref,
               gt_ref, o_ref, qbd_ref, kseq_ref, vt_ref, bias_ref, s_ref, acc_ref,
               *, tq, tk, seqs, lam_init):
    n_p, s_p, s_s = seqs
    h = pl.program_id(0)
    i = pl.program_id(1)
    t0 = i * tq
    lo, length = _seq_bounds(t0, seqs)
    qpos0 = t0 - lo
    tl = bias_ref.shape[1]
    origin = s_p - tl

    @pl.when(i == 0)
    def _():
        slope = slope_ref[pl.ds(h, 1), :][:, :1] * LOG2E
        rows = lax.broadcasted_iota(jnp.int32, (tk, tl), 0) - lax.broadcasted_iota(jnp.int32, (tk, tl), 1)

        def fill(c, carry):
            r0 = pl.multiple_of(c * tk, tk)
            bias_ref[pl.ds(r0, tk), :] = -slope * jnp.abs((rows + (r0 - origin)).astype(F32))
            return carry

        lax.fori_loop(0, bias_ref.shape[0] // tk, fill, 0)

    @pl.when(qpos0 == 0)
    def _():
        _load_sequence(length, klo_ref, khi_ref, vlo_ref, vhi_ref, kseq_ref, vt_ref, s_s, s_p > s_s)

    q = q_ref[...]
    lane = lax.broadcasted_iota(jnp.int32, q.shape, 1)
    zero = jnp.zeros_like(q)
    qbd_ref[0:tq, :] = jnp.where(lane < DQ_B, q, zero)
    qbd_ref[tq:2 * tq, :] = jnp.where(lane >= DQ_B, q, zero)
    acc_ref[...] = jnp.zeros_like(acc_ref)

    def score(j, slot):
        start = pl.multiple_of(j * tk, tk)
        st = lax.dot_general(kseq_ref[pl.ds(start, tk), :], qbd_ref[...], NT_DIMS,
                             preferred_element_type=F32)
        row0 = start - qpos0 + origin
        bias = [bias_ref[pl.ds(pl.multiple_of(row0 - g * tl, tl), tk), :] for g in range(tq // tl)]
        st = st + jnp.concatenate(bias + bias, axis=1)
        s_ref[slot] = st
        return jnp.max(st, axis=0, keepdims=True)

    def accumulate(j, slot, m_old, m_tile):
        m_new = jnp.maximum(m_old, m_tile)
        p = jnp.exp2(s_ref[slot] - m_new).astype(BF16)
        vt = vt_ref[:, pl.ds(pl.multiple_of(j * tk, tk), tk)]
        acc_ref[...] = (acc_ref[...] * jnp.exp2(m_old - m_new)
                        + jnp.dot(vt, p, preferred_element_type=F32))
        return m_new

    def pair(jj, carry):
        m_run, m_even = carry
        j = 2 * jj
        m_odd = score(j + 1, 1)
        m_run = accumulate(j, 0, m_run, m_even)
        m_even = score(j + 2, 0)
        m_run = accumulate(j + 1, 1, m_run, m_odd)
        return m_run, m_even

    n_pairs = length // (2 * tk)
    m_run = jnp.full((1, 2 * tq), NEG_INF, F32)
    m_run, m_even = lax.fori_loop(0, n_pairs - 1, pair, (m_run, score(0, 0)))
    j_last = 2 * (n_pairs - 1)
    m_odd = score(j_last + 1, 1)
    m_run = accumulate(j_last, 0, m_run, m_even)
    accumulate(j_last + 1, 1, m_run, m_odd)

    lam = (jnp.exp(jnp.sum(lq1_ref[...] * lk1_ref[...], axis=-1, keepdims=True))
           - jnp.exp(jnp.sum(lq2_ref[...] * lk2_ref[...], axis=-1, keepdims=True)) + lam_init)
    ot = acc_ref[:HEAD_DIM, :] / acc_ref[HEAD_DIM:HEAD_DIM + 1, :]
    _finish_head(ot[:, :tq] - lam * ot[:, tq:], gt_ref, 1.0 - lam_init, o_ref)


def _diff_attention(h, lq1, lk1, lq2, lk2, gt, layer, lam_init, tq, tk, seqs):
    t = h.shape[0]
    n_p, s_p, s_s = seqs
    hd = HEAD_DIM
    gq, gk, gv = 3 * N_HEADS_B, 4 * N_HEADS_B, 5 * N_HEADS_B
    slopes = jnp.broadcast_to(jnp.asarray(SLOPES_B, F32)[:, None], (N_HEADS_B, hd))
    lam_spec = pl.BlockSpec((None, 1, DQ_B), lambda hh, i: (layer, 0, 0))
    assert s_s % (2 * tk) == 0, "key tiles are processed in pairs"
    tl = min(tq, BIAS_LANES)
    assert tk % tl == 0 and tq % tl == 0 and s_p % tk == 0
    bias_rows = 2 * s_p
    est = 4 * s_s * hd * 2 + 2 * s_p * hd * 2 + bias_rows * tl * 4 + 8 * 2 * tq * tk * 4
    return pl.pallas_call(
        functools.partial(_diff_body, tq=tq, tk=tk, seqs=seqs, lam_init=lam_init),
        grid=(N_HEADS_B, t // tq),
        in_specs=[
            pl.BlockSpec((tq, hd), lambda hh, i: (i, gq + hh)),
            *_kv_specs(tq, seqs, gk, gv),
            pl.BlockSpec((N_HEADS_B, hd), lambda hh, i: (0, 0)),
            lam_spec, lam_spec, lam_spec, lam_spec,
            pl.BlockSpec((None, hd, 1), lambda hh, i: (layer, 0, 0)),
        ],
        out_specs=pl.BlockSpec((tq, hd), lambda hh, i: (i, hh)),
        out_shape=jax.ShapeDtypeStruct((t, N_HEADS_B * hd), BF16),
        scratch_shapes=[pltpu.VMEM((2 * tq, hd), BF16), pltpu.VMEM((s_p, hd), BF16),
                        pltpu.VMEM((hd + ONES_ROWS, s_p), BF16), pltpu.VMEM((bias_rows, tl), F32),
                        pltpu.VMEM((2, tk, 2 * tq), F32), pltpu.VMEM((hd + ONES_ROWS, 2 * tq), F32)],
        compiler_params=pltpu.CompilerParams(
            dimension_semantics=("arbitrary", "arbitrary"), vmem_limit_bytes=_vmem_limit(est)),
        name="diff_attn",
    )(h, h, h, h, h, slopes, lq1, lk1, lq2, lk2, gt)


V7X_VMEM_BYTES = 64 * MIB
V7X_VMEM_CEILING = V7X_VMEM_BYTES - 6 * MIB


def _vmem_limit(estimate_bytes):
    return int(min(max(2 * estimate_bytes, 32 * MIB), V7X_VMEM_CEILING))


def _largest_tile(n, cap, step):
    best = None
    for c in range(step, min(n, cap) + 1, step):
        if n % c == 0:
            best = c
    assert best is not None, (n, cap, step)
    return best


def _plan(t, d_ff, s_s):
    tq_dil = _largest_tile(s_s, 512, 128)
    return dict(
        tm=_largest_tile(t, 512, 128),
        tm_ffn=_largest_tile(t, 768, 128),
        tm_proj=_largest_tile(t, 1024, 128),
        tf=_largest_tile(d_ff, 512, 128),
        tq_dil=tq_dil,
        tk_dil=2 * tq_dil,
        n_sub_dil=_largest_tile(s_s // tq_dil, 8, 1),
        tq_diff=_largest_tile(s_s, 2048, 128),
        tk_diff=_largest_tile(s_s, 512, 128),
    )


def kernel(x_prompt, x_sample, ln1_g, ln1_b, ffn1_gate, ffn1_up, ffn1_down, w_in, norm_a_g, lam_q1, lam_k1,
           lam_q2, lam_k2, subln_g, w_out, ln2_g, ln2_b, ffn2_gate, ffn2_up, ffn2_down, ln3_g, ln3_b):
    n_p, s_p, d = x_prompt.shape
    n_s, s_s, _ = x_sample.shape
    depth = w_in.shape[0]
    assert w_in.shape[-1] == N_GROUPS * GROUP_WIDTH
    seqs = (n_p, s_p, s_s)
    p_tot = n_p * s_p
    x = jnp.concatenate([x_prompt.reshape(p_tot, d), x_sample.reshape(n_s * s_s, d)], axis=0)
    t = x.shape[0]
    plan = _plan(t, ffn1_gate.shape[-1], s_s)
    alpha = (2 * depth) ** 0.25

    row = lambda v: v.reshape(depth, 1, v.shape[-1])
    col = lambda v: v.reshape(depth, v.shape[-1], 1)
    bf = lambda v: v.astype(BF16)
    ffn1 = (bf(ffn1_gate), bf(ffn1_up), bf(ffn1_down), row(ln1_g), row(ln1_b))
    ffn2 = (bf(ffn2_gate), bf(ffn2_up), bf(ffn2_down), row(ln3_g), row(ln3_b))
    w_in_b, w_out_b = bf(w_in), bf(w_out)
    ln2 = (row(ln2_g), row(ln2_b))
    gat, gbt = col(norm_a_g), col(subln_g)
    lams = [row(v) for v in (lam_q1, lam_k1, lam_q2, lam_k2)]

    for i in range(depth):
        x = _ffn(x, *ffn1, i, alpha, plan["tm_ffn"], plan["tf"])
        h = _in_proj(x, w_in_b, i, plan["tm_proj"])
        ya = _dilated_attention(h, gat, i, plan["tq_dil"], plan["tk_dil"], plan["n_sub_dil"], seqs)
        lam_init = 0.8 - 0.6 * math.exp(-0.3 * i)
        yb = _diff_attention(h, *lams, gbt, i, lam_init, plan["tq_diff"], plan["tk_diff"], seqs)
        x = _out_proj(x, ya, yb, w_out_b, *ln2, i, alpha, plan["tm"])
        x = _ffn(x, *ffn2, i, alpha, plan["tm_ffn"], plan["tf"])
    return x[:p_tot].reshape(x_prompt.shape), x[p_tot:].reshape(x_sample.shape)
```

```python
import functools
import math

import jax
import jax.numpy as jnp
from jax import lax
from jax.experimental import pallas as pl
from jax.experimental.pallas import tpu as pltpu

F32 = jnp.float32
BF16 = jnp.bfloat16

HEAD_DIM = 128
N_HEADS_A = 8
N_HEADS_B = 8
DQ_B = HEAD_DIM // 2
GROUP_WIDTH = N_HEADS_A * HEAD_DIM
N_GROUPS = 6
DILATED_CONFIGS = ((128, 1), (512, 4), (2048, 16))
ALIBI_MAX_BIAS = 8.0
LN_EPS = 1e-5
RMS_EPS = 1e-6
NEG_INF = -1e30

LANES = 128
MIB = 1024 * 1024
NT_DIMS = (((1,), (1,)), ((), ()))
LOG2E = math.log2(math.e)
Q_SCALE_A = HEAD_DIM ** -0.5 * LOG2E
Q_SCALE_B = DQ_B ** -0.5 * LOG2E


def _alibi_slope(n):
    return 2.0 ** (-ALIBI_MAX_BIAS * n / (N_HEADS_A + N_HEADS_B))


SLOPES_A = tuple(_alibi_slope(2 * h + 1) for h in range(N_HEADS_A))
SLOPES_B = tuple(_alibi_slope(2 * h + 2) for h in range(N_HEADS_B))


def _layer_norm(y, g, b):
    mu = jnp.mean(y, axis=-1, keepdims=True)
    d = y - mu
    var = jnp.mean(d * d, axis=-1, keepdims=True)
    return d * lax.rsqrt(var + LN_EPS) * g + b


def _ffn_body(x_ref, wg_ref, wu_ref, wd_ref, g_ref, b_ref, o_ref, xb_ref, acc_ref, *, alpha):
    j = pl.program_id(1)

    @pl.when(j == 0)
    def _():
        xb_ref[...] = x_ref[...].astype(BF16)
        acc_ref[...] = jnp.zeros_like(acc_ref)

    xb = xb_ref[...]
    gate = jnp.dot(xb, wg_ref[...], preferred_element_type=F32)
    up = jnp.dot(xb, wu_ref[...], preferred_element_type=F32)
    act = (gate * jax.nn.sigmoid(gate) * up).astype(BF16)
    acc_ref[...] += jnp.dot(act, wd_ref[...], preferred_element_type=F32)

    @pl.when(j == pl.num_programs(1) - 1)
    def _():
        y = alpha * x_ref[...] + 0.5 * acc_ref[...]
        o_ref[...] = _layer_norm(y, g_ref[...], b_ref[...])


def _ffn(x, wg, wu, wd, g, b, layer, alpha, tm, tf):
    t, d = x.shape
    f = wg.shape[-1]
    est = 2 * 2 * tm * d * 4 + 2 * 3 * d * tf * 2 + tm * d * 6 + 4 * tm * tf * 4
    return pl.pallas_call(
        functools.partial(_ffn_body, alpha=alpha),
        grid=(t // tm, f // tf),
        in_specs=[
            pl.BlockSpec((tm, d), lambda i, j: (i, 0)),
            pl.BlockSpec((None, d, tf), lambda i, j: (layer, 0, j)),
            pl.BlockSpec((None, d, tf), lambda i, j: (layer, 0, j)),
            pl.BlockSpec((None, tf, d), lambda i, j: (layer, j, 0)),
            pl.BlockSpec((None, 1, d), lambda i, j: (layer, 0, 0)),
            pl.BlockSpec((None, 1, d), lambda i, j: (layer, 0, 0)),
        ],
        out_specs=pl.BlockSpec((tm, d), lambda i, j: (i, 0)),
        out_shape=jax.ShapeDtypeStruct((t, d), F32),
        scratch_shapes=[pltpu.VMEM((tm, d), BF16), pltpu.VMEM((tm, d), F32)],
        compiler_params=pltpu.CompilerParams(
            dimension_semantics=("parallel", "arbitrary"), vmem_limit_bytes=_vmem_limit(est)),
        name="ffn",
    )(x, wg, wu, wd, g, b)


def _in_proj_body(x_ref, w_ref, o_ref, xb_ref):
    j = pl.program_id(1)

    @pl.when(j == 0)
    def _():
        xb_ref[...] = x_ref[...].astype(BF16)

    scale = jnp.where(j == 0, Q_SCALE_A, jnp.where(j == N_GROUPS // 2, Q_SCALE_B, 1.0)).astype(F32)
    o_ref[...] = (jnp.dot(xb_ref[...], w_ref[...], preferred_element_type=F32) * scale).astype(BF16)


def _in_proj(x, w, layer, tm):
    t, d = x.shape
    n = w.shape[-1]
    tn = GROUP_WIDTH
    est = 2 * tm * d * 4 + 2 * d * tn * 2 + 2 * tm * tn * 2 + tm * d * 2 + tm * tn * 4
    return pl.pallas_call(
        _in_proj_body,
        grid=(t // tm, n // tn),
        in_specs=[
            pl.BlockSpec((tm, d), lambda i, j: (i, 0)),
            pl.BlockSpec((None, d, tn), lambda i, j: (layer, 0, j)),
        ],
        out_specs=pl.BlockSpec((tm, tn), lambda i, j: (i, j)),
        out_shape=jax.ShapeDtypeStruct((t, n), BF16),
        scratch_shapes=[pltpu.VMEM((tm, d), BF16)],
        compiler_params=pltpu.CompilerParams(
            dimension_semantics=("parallel", "arbitrary"), vmem_limit_bytes=_vmem_limit(est)),
        name="in_proj",
    )(x, w)


def _out_proj_body(x_ref, ya_ref, yb_ref, w_ref, g_ref, b_ref, o_ref, *, alpha):
    wa = ya_ref.shape[-1]
    rows = x_ref.shape[0] // 4
    for r in range(0, x_ref.shape[0], rows):
        rs = slice(r, r + rows)
        y = jnp.dot(ya_ref[rs, :], w_ref[:wa, :], preferred_element_type=F32)
        y = y + jnp.dot(yb_ref[rs, :], w_ref[wa:, :], preferred_element_type=F32)
        o_ref[rs, :] = _layer_norm(alpha * x_ref[rs, :] + y, g_ref[...], b_ref[...])


def _out_proj(x, ya, yb, w, g, b, layer, alpha, tm):
    t, d = x.shape
    wa, wb = ya.shape[-1], yb.shape[-1]
    est = 2 * 2 * tm * d * 4 + 2 * tm * (wa + wb) * 2 + 2 * (wa + wb) * d * 2 + 2 * tm * d * 4
    return pl.pallas_call(
        functools.partial(_out_proj_body, alpha=alpha),
        grid=(t // tm,),
        in_specs=[
            pl.BlockSpec((tm, d), lambda i: (i, 0)),
            pl.BlockSpec((tm, wa), lambda i: (i, 0)),
            pl.BlockSpec((tm, wb), lambda i: (i, 0)),
            pl.BlockSpec((None, wa + wb, d), lambda i: (layer, 0, 0)),
            pl.BlockSpec((None, 1, d), lambda i: (layer, 0, 0)),
            pl.BlockSpec((None, 1, d), lambda i: (layer, 0, 0)),
        ],
        out_specs=pl.BlockSpec((tm, d), lambda i: (i, 0)),
        out_shape=jax.ShapeDtypeStruct((t, d), F32),
        compiler_params=pltpu.CompilerParams(
            dimension_semantics=("parallel",), vmem_limit_bytes=_vmem_limit(est)),
        name="out_proj",
    )(x, ya, yb, w, g, b)


def _seq_bounds(t0, seqs):
    n_p, s_p, s_s = seqs
    p_tot = n_p * s_p
    in_prompt = t0 < p_tot
    lo_p = (t0 // s_p) * s_p
    lo_s = p_tot + ((jnp.maximum(t0, p_tot) - p_tot) // s_s) * s_s
    return jnp.where(in_prompt, lo_p, lo_s), jnp.where(in_prompt, s_p, s_s)


def _kv_block_maps(tq, seqs):
    n_p, s_p, s_s = seqs
    assert s_p in (s_s, 2 * s_s), "a sequence is held as at most two blocks of s_s rows"
    per_seq = s_p // s_s
    tiles_per_blk = s_s // tq
    p_blks = n_p * per_seq

    def lo_blk(i):
        b = i // tiles_per_blk
        return jnp.where(b < p_blks, (b // per_seq) * per_seq, b)

    def hi_blk(i):
        b = i // tiles_per_blk
        return jnp.where(b < p_blks, (b // per_seq) * per_seq + (per_seq - 1), b)

    return lo_blk, hi_blk


def _kv_specs(tq, seqs, k_col, v_col):
    s_s = seqs[2]
    lo_blk, hi_blk = _kv_block_maps(tq, seqs)

    def spec(blk, col):
        return pl.BlockSpec((s_s, HEAD_DIM), lambda hh, i: (blk(i), col + hh), pipeline_mode=pl.Buffered(1))

    return [spec(lo_blk, k_col), spec(hi_blk, k_col), spec(lo_blk, v_col), spec(hi_blk, v_col)]


ONES_ROWS = 16
BIAS_LANES = 256


def _load_sequence(length, klo_ref, khi_ref, vlo_ref, vhi_ref, kseq_ref, vt_ref, s_s, two_blocks):
    def fill(k_ref, v_ref, base):
        kseq_ref[base:base + s_s, :] = k_ref[...]
        vt_ref[HEAD_DIM:, base:base + s_s] = jnp.ones((ONES_ROWS, s_s), vt_ref.dtype)
        for c in range(s_s // LANES):
            rows = slice(c * LANES, (c + 1) * LANES)
            vt_ref[:HEAD_DIM, base + c * LANES:base + (c + 1) * LANES] = v_ref[rows, :].T

    fill(klo_ref, vlo_ref, 0)
    if two_blocks:
        @pl.when(length > s_s)
        def _():
            fill(khi_ref, vhi_ref, s_s)


def _finish_head(ot, gt_ref, gain, o_ref):
    ms = jnp.mean(ot * ot, axis=0, keepdims=True)
    yt = ot * lax.rsqrt(ms + RMS_EPS) * gt_ref[...]
    if gain != 1.0:
        yt = yt * gain
    o_ref[...] = yt.T.astype(o_ref.dtype)


def _dilated_windows(tq, tk):
    out = []
    for window, dil in DILATED_CONFIGS:
        reach = ((window // 2) // dil) * dil
        halo = -(-reach // BIAS_LANES) * BIAS_LANES
        while (tq + 2 * halo) % tk:
            halo += BIAS_LANES
        out.append((dil, reach, halo, tq + 2 * halo))
    return out


def _dilated_bias_rows(halo, width, tq, tl):
    return width + 2 * halo + tq - tl


def _dilated_body(q_ref, klo_ref, khi_ref, vlo_ref, vhi_ref, slope_ref, gt_ref, o_ref, kseq_ref, vt_ref,
                  bias_ref, s_ref, acc_ref, *, tq, tk, seqs):
    n_p, s_p, s_s = seqs
    n_sub = acc_ref.shape[0]
    h = pl.program_id(0)
    i = pl.program_id(1)
    t0 = i * (n_sub * tq)
    lo, length = _seq_bounds(t0, seqs)
    qpos0 = t0 - lo
    windows = _dilated_windows(tq, tk)

    tl = bias_ref.shape[1]
    bases, origins = [], []
    base = 0
    for dil, reach, halo, width in windows:
        bases.append(base)
        origins.append(2 * halo + tq - tl)
        base += _dilated_bias_rows(halo, width, tq, tl)

    @pl.when(i == 0)
    def _():
        slope = slope_ref[pl.ds(h, 1), :][:, :1] * LOG2E
        for (dil, reach, halo, width), b0, origin in zip(windows, bases, origins):
            shape = (_dilated_bias_rows(halo, width, tq, tl), tl)
            d = (lax.broadcasted_iota(jnp.int32, shape, 0) - lax.broadcasted_iota(jnp.int32, shape, 1)
                 - origin)
            dist = jnp.abs(d.astype(F32))
            valid = dist <= float(reach)
            if dil > 1:
                valid = valid & ((d & (dil - 1)) == 0)
            bias_ref[b0:b0 + shape[0], :] = jnp.where(valid, -slope * dist, NEG_INF)

    @pl.when(qpos0 == 0)
    def _():
        _load_sequence(length, klo_ref, khi_ref, vlo_ref, vhi_ref, kseq_ref, vt_ref, s_s, s_p > s_s)

    firsts = []
    per_query_tile = 0
    for dil, reach, halo, width in windows:
        firsts.append(per_query_tile)
        per_query_tile += width // tk
    n_tiles = n_sub * per_query_tile

    def offsets(j):
        sub = lax.div(j, per_query_tile)
        local = j - sub * per_query_tile
        qpos = qpos0 + sub * tq
        k_off = b_off = None
        for (dil, reach, halo, width), first, b0, origin in zip(windows, firsts, bases, origins):
            w0 = jnp.clip(qpos - halo, 0, length - width)
            k_br = w0 + (local - first) * tk
            b_br = b0 + w0 - qpos + origin + (local - first) * tk
            k_off = k_br if k_off is None else jnp.where(local >= first, k_br, k_off)
            b_off = b_br if b_off is None else jnp.where(local >= first, b_br, b_off)
        return sub, local, pl.multiple_of(k_off, tl), pl.multiple_of(b_off, tl)

    acc_ref[...] = jnp.zeros_like(acc_ref)

    def score(j, slot):
        sub, _, k_off, b_off = offsets(j)
        q = q_ref[pl.ds(pl.multiple_of(sub * tq, tq), tq), :]
        st = lax.dot_general(kseq_ref[pl.ds(k_off, tk), :], q, NT_DIMS, preferred_element_type=F32)
        bias = [bias_ref[pl.ds(pl.multiple_of(b_off - g * tl, tl), tk), :] for g in range(tq // tl)]
        st = st + jnp.concatenate(bias, axis=1)
        s_ref[slot] = st
        return jnp.max(st, axis=0, keepdims=True)

    def accumulate(j, slot, m_run, m_tile):
        sub, local, k_off, _ = offsets(j)
        m_old = jnp.where(local == 0, NEG_INF, m_run)
        m_new = jnp.maximum(m_old, m_tile)
        p = jnp.exp2(s_ref[slot] - m_new).astype(BF16)
        acc_ref[sub] = (acc_ref[sub] * jnp.exp2(m_old - m_new)
                        + jnp.dot(vt_ref[:, pl.ds(k_off, tk)], p, preferred_element_type=F32))
        return m_new

    def pair(jj, carry):
        m_run, m_even = carry
        j = 2 * jj
        m_odd = score(j + 1, 1)
        m_run = accumulate(j, 0, m_run, m_even)
        m_even = score(j + 2, 0)
        m_run = accumulate(j + 1, 1, m_run, m_odd)
        return m_run, m_even

    m_run = jnp.full((1, tq), NEG_INF, F32)
    if n_tiles % 2:
        m_run, m_last = lax.fori_loop(0, n_tiles // 2, pair, (m_run, score(0, 0)))
        accumulate(n_tiles - 1, 0, m_run, m_last)
    else:
        m_run, m_even = lax.fori_loop(0, n_tiles // 2 - 1, pair, (m_run, score(0, 0)))
        m_odd = score(n_tiles - 1, 1)
        m_run = accumulate(n_tiles - 2, 0, m_run, m_even)
        accumulate(n_tiles - 1, 1, m_run, m_odd)
    for sub in range(n_sub):
        _finish_head(acc_ref[sub, :HEAD_DIM, :] / acc_ref[sub, HEAD_DIM:HEAD_DIM + 1, :], gt_ref, 1.0,
                     o_ref.at[sub * tq:(sub + 1) * tq, :])


def _dilated_attention(h, gt, layer, tq, tk, n_sub, seqs):
    t = h.shape[0]
    n_p, s_p, s_s = seqs
    hd = HEAD_DIM
    tq_step = n_sub * tq
    assert s_s % tq_step == 0
    for dil, _, _, width in _dilated_windows(tq, tk):
        assert dil & (dil - 1) == 0 and width <= s_s and width % tk == 0
    gq, gk, gv = 0, N_HEADS_A, 2 * N_HEADS_A
    slopes = jnp.broadcast_to(jnp.asarray(SLOPES_A, F32)[:, None], (N_HEADS_A, hd))
    tl = min(tq, BIAS_LANES)
    assert tq % tl == 0 and tk % tl == 0 and tl == BIAS_LANES
    bias_rows = sum(_dilated_bias_rows(halo, width, tq, tl) for _, _, halo, width in _dilated_windows(tq, tk))
    est = 4 * s_s * hd * 2 + 2 * s_p * hd * 2 + bias_rows * tl * 4 + 8 * tk * tq * 4
    return pl.pallas_call(
        functools.partial(_dilated_body, tq=tq, tk=tk, seqs=seqs),
        grid=(N_HEADS_A, t // tq_step),
        in_specs=[
            pl.BlockSpec((tq_step, hd), lambda hh, i: (i, gq + hh)),
            *_kv_specs(tq_step, seqs, gk, gv),
            pl.BlockSpec((N_HEADS_A, hd), lambda hh, i: (0, 0)),
            pl.BlockSpec((None, hd, 1), lambda hh, i: (layer, 0, 0)),
        ],
        out_specs=pl.BlockSpec((tq_step, hd), lambda hh, i: (i, hh)),
        out_shape=jax.ShapeDtypeStruct((t, N_HEADS_A * hd), BF16),
        scratch_shapes=[pltpu.VMEM((s_p, hd), BF16), pltpu.VMEM((hd + ONES_ROWS, s_p), BF16),
                        pltpu.VMEM((bias_rows, tl), F32), pltpu.VMEM((2, tk, tq), F32),
                        pltpu.VMEM((n_sub, hd + ONES_ROWS, tq), F32)],
        compiler_params=pltpu.CompilerParams(
            dimension_semantics=("arbitrary", "arbitrary"), vmem_limit_bytes=_vmem_limit(est)),
        name="dilated_attn",
    )(h, h, h, h, h, slopes, gt)


def _diff_body(q_ref, klo_ref, khi_ref, vlo_ref, vhi_ref, slope_ref, lq1_ref, lk1_ref, lq2_ref, lk2_ref,
               gt_ref, o_ref, qbd_ref, kseq_ref, vt_ref, bias_ref, s_ref, acc_ref,
               *, tq, tk, seqs, lam_init):
    n_p, s_p, s_s = seqs
    h = pl.program_id(0)
    i = pl.program_id(1)
    t0 = i * tq
    lo, length = _seq_bounds(t0, seqs)
    qpos0 = t0 - lo
    tl = bias_ref.shape[1]
    origin = s_p - tl

    @pl.when(i == 0)
    def _():
        slope = slope_ref[pl.ds(h, 1), :][:, :1] * LOG2E
        rows = lax.broadcasted_iota(jnp.int32, (tk, tl), 0) - lax.broadcasted_iota(jnp.int32, (tk, tl), 1)

        def fill(c, carry):
            r0 = pl.multiple_of(c * tk, tk)
            bias_ref[pl.ds(r0, tk), :] = -slope * jnp.abs((rows + (r0 - origin)).astype(F32))
            return carry

        lax.fori_loop(0, bias_ref.shape[0] // tk, fill, 0)

    @pl.when(qpos0 == 0)
    def _():
        _load_sequence(length, klo_ref, khi_ref, vlo_ref, vhi_ref, kseq_ref, vt_ref, s_s, s_p > s_s)

    q = q_ref[...]
    lane = lax.broadcasted_iota(jnp.int32, q.shape, 1)
    zero = jnp.zeros_like(q)
    qbd_ref[0:tq, :] = jnp.where(lane < DQ_B, q, zero)
    qbd_ref[tq:2 * tq, :] = jnp.where(lane >= DQ_B, q, zero)
    acc_ref[...] = jnp.zeros_like(acc_ref)

    def score(j, slot):
        start = pl.multiple_of(j * tk, tk)
        st = lax.dot_general(kseq_ref[pl.ds(start, tk), :], qbd_ref[...], NT_DIMS,
                             preferred_element_type=F32)
        row0 = start - qpos0 + origin
        bias = [bias_ref[pl.ds(pl.multiple_of(row0 - g * tl, tl), tk), :] for g in range(tq // tl)]
        st = st + jnp.concatenate(bias + bias, axis=1)
        s_ref[slot] = st
        return jnp.max(st, axis=0, keepdims=True)

    def accumulate(j, slot, m_old, m_tile):
        m_new = jnp.maximum(m_old, m_tile)
        p = jnp.exp2(s_ref[slot] - m_new).astype(BF16)
        vt = vt_ref[:, pl.ds(pl.multiple_of(j * tk, tk), tk)]
        acc_ref[...] = (acc_ref[...] * jnp.exp2(m_old - m_new)
                        + jnp.dot(vt, p, preferred_element_type=F32))
        return m_new

    def pair(jj, carry):
        m_run, m_even = carry
        j = 2 * jj
        m_odd = score(j + 1, 1)
        m_run = accumulate(j, 0, m_run, m_even)
        m_even = score(j + 2, 0)
        m_run = accumulate(j + 1, 1, m_run, m_odd)
        return m_run, m_even

    n_pairs = length // (2 * tk)
    m_run = jnp.full((1, 2 * tq), NEG_INF, F32)
    m_run, m_even = lax.fori_loop(0, n_pairs - 1, pair, (m_run, score(0, 0)))
    j_last = 2 * (n_pairs - 1)
    m_odd = score(j_last + 1, 1)
    m_run = accumulate(j_last, 0, m_run, m_even)
    accumulate(j_last + 1, 1, m_run, m_odd)

    lam = (jnp.exp(jnp.sum(lq1_ref[...] * lk1_ref[...], axis=-1, keepdims=True))
           - jnp.exp(jnp.sum(lq2_ref[...] * lk2_ref[...], axis=-1, keepdims=True)) + lam_init)
    ot = acc_ref[:HEAD_DIM, :] / acc_ref[HEAD_DIM:HEAD_DIM + 1, :]
    _finish_head(ot[:, :tq] - lam * ot[:, tq:], gt_ref, 1.0 - lam_init, o_ref)


def _diff_attention(h, lq1, lk1, lq2, lk2, gt, layer, lam_init, tq, tk, seqs):
    t = h.shape[0]
    n_p, s_p, s_s = seqs
    hd = HEAD_DIM
    gq, gk, gv = 3 * N_HEADS_B, 4 * N_HEADS_B, 5 * N_HEADS_B
    slopes = jnp.broadcast_to(jnp.asarray(SLOPES_B, F32)[:, None], (N_HEADS_B, hd))
    lam_spec = pl.BlockSpec((None, 1, DQ_B), lambda hh, i: (layer, 0, 0))
    assert s_s % (2 * tk) == 0, "key tiles are processed in pairs"
    tl = min(tq, BIAS_LANES)
    assert tk % tl == 0 and tq % tl == 0 and s_p % tk == 0
    bias_rows = 2 * s_p
    est = 4 * s_s * hd * 2 + 2 * s_p * hd * 2 + bias_rows * tl * 4 + 8 * 2 * tq * tk * 4
    return pl.pallas_call(
        functools.partial(_diff_body, tq=tq, tk=tk, seqs=seqs, lam_init=lam_init),
        grid=(N_HEADS_B, t // tq),
        in_specs=[
            pl.BlockSpec((tq, hd), lambda hh, i: (i, gq + hh)),
            *_kv_specs(tq, seqs, gk, gv),
            pl.BlockSpec((N_HEADS_B, hd), lambda hh, i: (0, 0)),
            lam_spec, lam_spec, lam_spec, lam_spec,
            pl.BlockSpec((None, hd, 1), lambda hh, i: (layer, 0, 0)),
        ],
        out_specs=pl.BlockSpec((tq, hd), lambda hh, i: (i, hh)),
        out_shape=jax.ShapeDtypeStruct((t, N_HEADS_B * hd), BF16),
        scratch_shapes=[pltpu.VMEM((2 * tq, hd), BF16), pltpu.VMEM((s_p, hd), BF16),
                        pltpu.VMEM((hd + ONES_ROWS, s_p), BF16), pltpu.VMEM((bias_rows, tl), F32),
                        pltpu.VMEM((2, tk, 2 * tq), F32), pltpu.VMEM((hd + ONES_ROWS, 2 * tq), F32)],
        compiler_params=pltpu.CompilerParams(
            dimension_semantics=("arbitrary", "arbitrary"), vmem_limit_bytes=_vmem_limit(est)),
        name="diff_attn",
    )(h, h, h, h, h, slopes, lq1, lk1, lq2, lk2, gt)


V7X_VMEM_BYTES = 64 * MIB
V7X_VMEM_CEILING = V7X_VMEM_BYTES - 6 * MIB


def _vmem_limit(estimate_bytes):
    return int(min(max(2 * estimate_bytes, 32 * MIB), V7X_VMEM_CEILING))


def _largest_tile(n, cap, step):
    best = None
    for c in range(step, min(n, cap) + 1, step):
        if n % c == 0:
            best = c
    assert best is not None, (n, cap, step)
    return best


def _plan(t, d_ff, s_s):
    tq_dil = _largest_tile(s_s, 512, 128)
    return dict(
        tm=_largest_tile(t, 512, 128),
        tm_ffn=_largest_tile(t, 768, 128),
        tm_proj=_largest_tile(t, 1024, 128),
        tf=_largest_tile(d_ff, 512, 128),
        tq_dil=tq_dil,
        tk_dil=2 * tq_dil,
        n_sub_dil=_largest_tile(s_s // tq_dil, 8, 1),
        tq_diff=_largest_tile(s_s, 2048, 128),
        tk_diff=_largest_tile(s_s, 512, 128),
    )


def kernel(x_prompt, x_sample, ln1_g, ln1_b, ffn1_gate, ffn1_up, ffn1_down, w_in, norm_a_g, lam_q1, lam_k1,
           lam_q2, lam_k2, subln_g, w_out, ln2_g, ln2_b, ffn2_gate, ffn2_up, ffn2_down, ln3_g, ln3_b):
    n_p, s_p, d = x_prompt.shape
    n_s, s_s, _ = x_sample.shape
    depth = w_in.shape[0]
    assert w_in.shape[-1] == N_GROUPS * GROUP_WIDTH
    seqs = (n_p, s_p, s_s)
    p_tot = n_p * s_p
    x = jnp.concatenate([x_prompt.reshape(p_tot, d), x_sample.reshape(n_s * s_s, d)], axis=0)
    t = x.shape[0]
    plan = _plan(t, ffn1_gate.shape[-1], s_s)
    alpha = (2 * depth) ** 0.25

    row = lambda v: v.reshape(depth, 1, v.shape[-1])
    col = lambda v: v.reshape(depth, v.shape[-1], 1)
    bf = lambda v: v.astype(BF16)
    ffn1 = (bf(ffn1_gate), bf(ffn1_up), bf(ffn1_down), row(ln1_g), row(ln1_b))
    ffn2 = (bf(ffn2_gate), bf(ffn2_up), bf(ffn2_down), row(ln3_g), row(ln3_b))
    w_in_b, w_out_b = bf(w_in), bf(w_out)
    ln2 = (row(ln2_g), row(ln2_b))
    gat, gbt = col(norm_a_g), col(subln_g)
    lams = [row(v) for v in (lam_q1, lam_k1, lam_q2, lam_k2)]

    for i in range(depth):
        x = _ffn(x, *ffn1, i, alpha, plan["tm_ffn"], plan["tf"])
        h = _in_proj(x, w_in_b, i, plan["tm_proj"])
        ya = _dilated_attention(h, gat, i, plan["tq_dil"], plan["tk_dil"], plan["n_sub_dil"], seqs)
        lam_init = 0.8 - 0.6 * math.exp(-0.3 * i)
        yb = _diff_attention(h, *lams, gbt, i, lam_init, plan["tq_diff"], plan["tk_diff"], seqs)
        x = _out_proj(x, ya, yb, w_out_b, *ln2, i, alpha, plan["tm"])
        x = _ffn(x, *ffn2, i, alpha, plan["tm_ffn"], plan["tf"])
    return x[:p_tot].reshape(x_prompt.shape), x[p_tot:].reshape(x_sample.shape)
```
